```python
import math
import jax, jax.numpy as jnp
from jax import lax
import numpy as np

D_MODEL = 1024
BATCH = 8
SEQ = 2048
DEPTH = 4
DEC_BATCH = 16
DEC_SEQ = 64
PAST_LEN = 2048

CHUNK = 64
HEAD_DIM = 64
RW_HEADS = 6
SB_HEADS = 6
CONV_GROUPS = 4
RW_DIM = RW_HEADS * HEAD_DIM
SB_DIM = SB_HEADS * HEAD_DIM
CONV_DIM = CONV_GROUPS * HEAD_DIM
MIX_DIM = RW_DIM + SB_DIM + CONV_DIM
IN_DIM = 3 * MIX_DIM
CONV_W = 3
FFN_DIM = 2816
N_MEM = 256
X_HEADS = 4
X_HEAD_DIM = D_MODEL // X_HEADS
DECAY_LORA = 64
A_LORA = 64
GATE_LORA = 128
QBLOCK = 128
RMS_EPS = 1e-6
GN_EPS = 64e-5
L2_EPS = 1e-12

kernel_name = "hymba_rwkv7_stickbreak_shortconv_macaron"


def rmsnorm(x, g):
    xf = x.astype(jnp.float32)
    y = xf * lax.rsqrt(jnp.mean(xf * xf, axis=-1, keepdims=True) + RMS_EPS)
    return (y * g.astype(jnp.float32)).astype(x.dtype)


def swiglu(h, wg, wu, wd):
    return (jax.nn.silu(h @ wg) * (h @ wu)) @ wd


def shift_right(seq, prev_row):
    return jnp.concatenate([prev_row[:, None].astype(seq.dtype), seq[:, :-1]], axis=1)


def rwkv7_mix(h, h_prev, rkv, rkv_prev, S0, p):
    B, T, _ = h.shape
    f32 = jnp.float32
    rkv = rkv + p["rw_mu_rkv"].reshape(3 * RW_DIM) * (shift_right(rkv, rkv_prev) - rkv)
    r, k, v = jnp.split(rkv, 3, axis=-1)
    dx = shift_right(h, h_prev) - h
    mu = p["rw_mu_wag"]
    xw = h + mu[0] * dx
    xa = h + mu[1] * dx
    xg = h + mu[2] * dx
    w = -jax.nn.softplus(-(p["rw_w0"] + jnp.tanh(xw @ p["rw_w1"]) @ p["rw_w2"])) - 0.5
    decay = jnp.exp(-jnp.exp(w.astype(f32)))
    a = jax.nn.sigmoid(p["rw_a0"] + (xa @ p["rw_a1"]) @ p["rw_a2"])
    g = jax.nn.sigmoid(xg @ p["rw_g1"]) @ p["rw_g2"]

    def heads(t):
        return t.reshape(B, T, RW_HEADS, HEAD_DIM).astype(f32)

    kk = heads(k * p["rw_k_k"])
    kk = kk * lax.rsqrt(jnp.sum(kk * kk, axis=-1, keepdims=True) + L2_EPS)
    k = k * (1.0 + (a - 1.0) * p["rw_k_a"])
    rh, kh, vh, ah, wh = heads(r), heads(k), heads(v), heads(a), heads(decay)

    def step(S, inp):
        r_t, k_t, v_t, kk_t, a_t, w_t = inp
        sa = jnp.einsum("bhvk,bhk->bhv", S, -kk_t)
        S = (S * w_t[:, :, None, :]
             + sa[..., None] * (kk_t * a_t)[:, :, None, :]
             + v_t[..., None] * k_t[:, :, None, :])
        return S, jnp.einsum("bhvk,bhk->bhv", S, r_t)

    xs = tuple(jnp.swapaxes(t, 0, 1) for t in (rh, kh, vh, kk, ah, wh))
    S_T, o = lax.scan(step, S0.astype(f32), xs)
    o = jnp.swapaxes(o, 0, 1)
    mean = jnp.mean(o, axis=-1, keepdims=True)
    var = jnp.mean(jnp.square(o - mean), axis=-1, keepdims=True)
    o = ((o - mean) * lax.rsqrt(var + GN_EPS)).reshape(B, T, RW_DIM)
    o = o * p["rw_ln_w"] + p["rw_ln_b"]
    bonus = (jnp.sum(rh * kh * p["rw_r_k"].astype(f32), axis=-1, keepdims=True) * vh).reshape(B, T, RW_DIM)
    out = (o + bonus) * g.astype(f32)
    return out.astype(h.dtype), S_T.astype(S0.dtype)


def stick_breaking(q, k, v, q_start):
    B, Tq, H, d = q.shape
    Tk = k.shape[1]
    blk = QBLOCK if Tq % QBLOCK == 0 else Tq
    nb = Tq // blk
    qb = jnp.swapaxes(q.reshape(B, nb, blk, H, d), 0, 1)
    kpos = jnp.arange(Tk)
    scale = HEAD_DIM ** -0.5

    def one_block(args):
        qi, bi = args
        z = jnp.einsum("bqhd,bkhd->bhqk", qi, k).astype(jnp.float32) * scale
        qpos = q_start + bi * blk + jnp.arange(blk)
        mask = kpos[None, :] < qpos[:, None]
        log_beta = jax.nn.log_sigmoid(z)
        log_rest = jnp.where(mask, jax.nn.log_sigmoid(-z), 0.0)
        tail = lax.cumsum(log_rest, axis=3, reverse=True) - log_rest
        wts = jnp.where(mask, jnp.exp(log_beta + tail), 0.0)
        return jnp.einsum("bhqk,bkhd->bqhd", wts.astype(v.dtype), v)

    out = lax.map(one_block, (qb, jnp.arange(nb)))
    return jnp.swapaxes(out, 0, 1).reshape(B, Tq, H, d)


def causal_short_conv(u, buf, w):
    T = u.shape[1]
    full = jnp.concatenate([buf.astype(u.dtype), u], axis=1)
    y = full[:, 0:T] * w[0]
    for i in range(1, CONV_W):
        y = y + full[:, i:i + T] * w[i]
    return y, full[:, T:]


def memory_kv(mem, p):
    B, M, _ = mem.shape
    m = rmsnorm(mem, p["mem_norm"])
    mk = rmsnorm((m @ p["x_wk"]).reshape(B, M, X_HEADS, X_HEAD_DIM), p["x_k_norm"])
    mv = (m @ p["x_wv"]).reshape(B, M, X_HEADS, X_HEAD_DIM)
    return mk, mv


def cross_attn(hx, mem_k, mem_v, p):
    B, T, _ = hx.shape
    q = rmsnorm((hx @ p["x_wq"]).reshape(B, T, X_HEADS, X_HEAD_DIM), p["x_q_norm"])
    s = jnp.einsum("bqhd,bkhd->bhqk", q, mem_k.astype(q.dtype)).astype(jnp.float32) * (X_HEAD_DIM ** -0.5)
    a = jax.nn.softmax(s, axis=-1).astype(hx.dtype)
    o = jnp.einsum("bhqk,bkhd->bqhd", a, mem_v.astype(hx.dtype)).reshape(B, T, D_MODEL)
    return o @ p["x_wo"]


def layer(x, mem_k, mem_v, sb_k_past, sb_v_past, S0, shift0, conv0, p):
    B, T, _ = x.shape
    x = x + 0.5 * swiglu(rmsnorm(x, p["ffn1_norm"]), p["ffn1_wg"], p["ffn1_wu"], p["ffn1_wd"])
    h = rmsnorm(x, p["mix_norm"])
    proj = h @ p["w_in"]
    rkv = proj[..., :3 * RW_DIM]
    sbq, sbk, sbv = jnp.split(proj[..., 3 * RW_DIM:3 * (RW_DIM + SB_DIM)], 3, axis=-1)
    cb, cc, cx = jnp.split(proj[..., 3 * (RW_DIM + SB_DIM):], 3, axis=-1)
    shift0 = shift0.astype(h.dtype)
    rkv_prev = shift0 @ p["w_in"][:, :3 * RW_DIM]
    o_rw, S_T = rwkv7_mix(h, shift0, rkv, rkv_prev, S0, p)
    q = rmsnorm(sbq.reshape(B, T, SB_HEADS, HEAD_DIM), p["sb_q_norm"])
    k_new = rmsnorm(sbk.reshape(B, T, SB_HEADS, HEAD_DIM), p["sb_k_norm"])
    v_new = sbv.reshape(B, T, SB_HEADS, HEAD_DIM)
    k_all = jnp.concatenate([sb_k_past.astype(k_new.dtype), k_new], axis=1)
    v_all = jnp.concatenate([sb_v_past.astype(v_new.dtype), v_new], axis=1)
    o_sb = stick_breaking(q, k_all, v_all, sb_k_past.shape[1]).reshape(B, T, SB_DIM)
    conv_out, conv_buf = causal_short_conv(cc * cx, conv0, p["cv_w"])
    o_cv = cb * conv_out
    x = x + jnp.concatenate([o_rw, o_sb, o_cv], axis=-1) @ p["w_out"]
    x = x + cross_attn(rmsnorm(x, p["x_norm"]), mem_k, mem_v, p)
    x = x + 0.5 * swiglu(rmsnorm(x, p["ffn2_norm"]), p["ffn2_wg"], p["ffn2_wu"], p["ffn2_wd"])
    return x, k_new, v_new, S_T, h[:, -1], conv_buf


def setup_inputs(seed: int = 0) -> dict:
    key = jax.random.key(seed)
    ks = iter(jax.random.split(key, 64))
    f32 = jnp.float32
    L = DEPTH

    def nrm(shape, scale=1.0):
        return jax.random.normal(next(ks), shape, f32) * scale

    def gain(shape):
        return 1.0 + 0.01 * jax.random.normal(next(ks), shape, f32)

    def unif(shape):
        return jax.random.uniform(next(ks), shape, f32)

    return {
        "x_prompt": nrm((BATCH, SEQ, D_MODEL)),
        "x_sample": nrm((DEC_BATCH, DEC_SEQ, D_MODEL)),
        "mem_prompt": nrm((BATCH, N_MEM, D_MODEL)),
        "cache_sb_k": nrm((L, DEC_BATCH, PAST_LEN, SB_HEADS, HEAD_DIM)),
        "cache_sb_v": nrm((L, DEC_BATCH, PAST_LEN, SB_HEADS, HEAD_DIM)),
        "state_rwkv": nrm((L, DEC_BATCH, RW_HEADS, HEAD_DIM, HEAD_DIM), 0.5),
        "state_shift": nrm((L, DEC_BATCH, D_MODEL)),
        "state_conv": nrm((L, DEC_BATCH, CONV_W - 1, CONV_DIM)),
        "cache_mem_k": nrm((L, DEC_BATCH, N_MEM, X_HEADS, X_HEAD_DIM)),
        "cache_mem_v": nrm((L, DEC_BATCH, N_MEM, X_HEADS, X_HEAD_DIM)),
        "ffn1_norm": gain((L, D_MODEL)),
        "ffn1_wg": nrm((L, D_MODEL, FFN_DIM), D_MODEL ** -0.5),
        "ffn1_wu": nrm((L, D_MODEL, FFN_DIM), D_MODEL ** -0.5),
        "ffn1_wd": nrm((L, FFN_DIM, D_MODEL), FFN_DIM ** -0.5),
        "mix_norm": gain((L, D_MODEL)),
        "w_in": nrm((L, D_MODEL, IN_DIM), D_MODEL ** -0.5),
        "w_out": nrm((L, MIX_DIM, D_MODEL), MIX_DIM ** -0.5),
        "rw_mu_rkv": unif((L, 3, RW_DIM)),
        "rw_mu_wag": unif((L, 3, D_MODEL)),
        "rw_w0": -0.5 + 0.5 * nrm((L, RW_DIM)),
        "rw_w1": nrm((L, D_MODEL, DECAY_LORA), D_MODEL ** -0.5),
        "rw_w2": nrm((L, DECAY_LORA, RW_DIM), 0.1),
        "rw_a0": nrm((L, RW_DIM), 0.1),
        "rw_a1": nrm((L, D_MODEL, A_LORA), D_MODEL ** -0.5),
        "rw_a2": nrm((L, A_LORA, RW_DIM), 0.1),
        "rw_g1": nrm((L, D_MODEL, GATE_LORA), D_MODEL ** -0.5),
        "rw_g2": nrm((L, GATE_LORA, RW_DIM), GATE_LORA ** -0.5),
        "rw_k_k": gain((L, RW_DIM)),
        "rw_k_a": gain((L, RW_DIM)),
        "rw_r_k": nrm((L, RW_HEADS, HEAD_DIM), 0.1),
        "rw_ln_w": gain((L, RW_DIM)),
        "rw_ln_b": nrm((L, RW_DIM), 0.01),
        "sb_q_norm": gain((L, HEAD_DIM)),
        "sb_k_norm": gain((L, HEAD_DIM)),
        "cv_w": nrm((L, CONV_W, CONV_DIM), CONV_W ** -0.5),
        "x_norm": gain((L, D_MODEL)),
        "mem_norm": gain((L, D_MODEL)),
        "x_wq": nrm((L, D_MODEL, D_MODEL), D_MODEL ** -0.5),
        "x_wk": nrm((L, D_MODEL, D_MODEL), D_MODEL ** -0.5),
        "x_wv": nrm((L, D_MODEL, D_MODEL), D_MODEL ** -0.5),
        "x_wo": nrm((L, D_MODEL, D_MODEL), D_MODEL ** -0.5),
        "x_q_norm": gain((L, X_HEAD_DIM)),
        "x_k_norm": gain((L, X_HEAD_DIM)),
        "ffn2_norm": gain((L, D_MODEL)),
        "ffn2_wg": nrm((L, D_MODEL, FFN_DIM), D_MODEL ** -0.5),
        "ffn2_wu": nrm((L, D_MODEL, FFN_DIM), D_MODEL ** -0.5),
        "ffn2_wd": nrm((L, FFN_DIM, D_MODEL), FFN_DIM ** -0.5),
    }


def reference(x_prompt, x_sample, mem_prompt, cache_sb_k, cache_sb_v, state_rwkv, state_shift,
              state_conv, cache_mem_k, cache_mem_v,
              ffn1_norm, ffn1_wg, ffn1_wu, ffn1_wd, mix_norm, w_in, w_out,
              rw_mu_rkv, rw_mu_wag, rw_w0, rw_w1, rw_w2, rw_a0, rw_a1, rw_a2, rw_g1, rw_g2,
              rw_k_k, rw_k_a, rw_r_k, rw_ln_w, rw_ln_b, sb_q_norm, sb_k_norm, cv_w,
              x_norm, mem_norm, x_wq, x_wk, x_wv, x_wo, x_q_norm, x_k_norm,
              ffn2_norm, ffn2_wg, ffn2_wu, ffn2_wd):
    params = dict(
        ffn1_norm=ffn1_norm, ffn1_wg=ffn1_wg, ffn1_wu=ffn1_wu, ffn1_wd=ffn1_wd,
        mix_norm=mix_norm, w_in=w_in, w_out=w_out,
        rw_mu_rkv=rw_mu_rkv, rw_mu_wag=rw_mu_wag, rw_w0=rw_w0, rw_w1=rw_w1, rw_w2=rw_w2,
        rw_a0=rw_a0, rw_a1=rw_a1, rw_a2=rw_a2, rw_g1=rw_g1, rw_g2=rw_g2,
        rw_k_k=rw_k_k, rw_k_a=rw_k_a, rw_r_k=rw_r_k, rw_ln_w=rw_ln_w, rw_ln_b=rw_ln_b,
        sb_q_norm=sb_q_norm, sb_k_norm=sb_k_norm, cv_w=cv_w,
        x_norm=x_norm, mem_norm=mem_norm, x_wq=x_wq, x_wk=x_wk, x_wv=x_wv, x_wo=x_wo,
        x_q_norm=x_q_norm, x_k_norm=x_k_norm,
        ffn2_norm=ffn2_norm, ffn2_wg=ffn2_wg, ffn2_wu=ffn2_wu, ffn2_wd=ffn2_wd,
    )
    dt = x_prompt.dtype
    Bp = x_prompt.shape[0]
    empty_kv = jnp.zeros((Bp, 0, SB_HEADS, HEAD_DIM), dt)
    zero_S = jnp.zeros((Bp, RW_HEADS, HEAD_DIM, HEAD_DIM), state_rwkv.dtype)
    zero_shift = jnp.zeros((Bp, D_MODEL), dt)
    zero_conv = jnp.zeros((Bp, CONV_W - 1, CONV_DIM), dt)

    yp, ys = x_prompt, x_sample
    pk, pv, pS, psh, pcv, pmk, pmv = [], [], [], [], [], [], []
    sk, sv, sS, ssh, scv = [], [], [], [], []
    for l in range(DEPTH):
        p = {n: a[l] for n, a in params.items()}
        mk, mv = memory_kv(mem_prompt, p)
        yp, kn, vn, S_T, sh, cbuf = layer(yp, mk, mv, empty_kv, empty_kv, zero_S, zero_shift, zero_conv, p)
        pk.append(kn); pv.append(vn); pS.append(S_T); psh.append(sh); pcv.append(cbuf)
        pmk.append(mk); pmv.append(mv)
        ys, kn, vn, S_T, sh, cbuf = layer(ys, cache_mem_k[l], cache_mem_v[l], cache_sb_k[l], cache_sb_v[l],
                                          state_rwkv[l], state_shift[l], state_conv[l], p)
        sk.append(kn); sv.append(vn); sS.append(S_T); ssh.append(sh); scv.append(cbuf)

    p_sb_k = jnp.stack(pk)
    p_sb_v = jnp.stack(pv)
    p_rwkv = jnp.stack(pS)
    p_shift = jnp.stack(psh)
    p_conv = jnp.stack(pcv)
    p_mem_k = jnp.stack(pmk)
    p_mem_v = jnp.stack(pmv)
    s_sb_k = jnp.stack(sk)
    s_sb_v = jnp.stack(sv)
    s_rwkv = jnp.stack(sS)
    s_shift = jnp.stack(ssh)
    s_conv = jnp.stack(scv)
    return (yp, ys, p_sb_k, p_sb_v, p_rwkv, p_shift, p_conv, p_mem_k, p_mem_v,
            s_sb_k, s_sb_v, s_rwkv, s_shift, s_conv)
```

```python
import functools

import jax
import jax.numpy as jnp
from jax import lax
from jax.experimental import pallas as pl
from jax.experimental.pallas import tpu as pltpu

F32 = jnp.float32
BF16 = jnp.bfloat16

D_MODEL = 1024
DEPTH = 4
HEAD_DIM = 64
RW_HEADS = 6
SB_HEADS = 6
RW_DIM = RW_HEADS * HEAD_DIM
SB_DIM = SB_HEADS * HEAD_DIM
CONV_DIM = 256
CONV_W = 3
FFN_DIM = 2816
X_HEADS = 4
X_HEAD_DIM = D_MODEL // X_HEADS
RMS_EPS = 1e-6
GN_EPS = 64e-5
L2_EPS = 1e-12

CHUNK = 64
LANES = 128
SUBLANES = 8
VMEM_LIMIT = 56 * 1024 * 1024

FFN_TM = 1024
FFN_TF = 256
MIX_TM = 512
RWKV_TM = 256
SB_TQ = 128
SB_TKP = 256
XA_TQ = 512
MEM_TM = 512


def _dot(a, b):
    return jnp.dot(a, b, preferred_element_type=F32)


def _dot_nt(a, b):
    return lax.dot_general(a, b, (((1,), (1,)), ((), ())), preferred_element_type=F32)


def _dot_tn(a, b):
    return lax.dot_general(a, b, (((0,), (0,)), ((), ())), preferred_element_type=F32)


def _split(x, terms):
    parts = []
    rem = x
    for t in range(terms):
        p = rem.astype(BF16)
        parts.append(p)
        if t + 1 < terms:
            rem = rem - p.astype(F32)
    return parts


def _dot_exact_rhs(x, m, terms=3):
    acc = None
    for p in _split(x, terms):
        y = _dot(p, m)
        acc = y if acc is None else acc + y
    return acc


def _dot_exact_lhs(m, x, terms=3):
    acc = None
    for p in _split(x, terms):
        y = _dot(m, p)
        acc = y if acc is None else acc + y
    return acc


def _rms(x, g):
    return x * lax.rsqrt(jnp.mean(x * x, axis=-1, keepdims=True) + RMS_EPS) * g


def _sigmoid(x):
    return 1.0 / (1.0 + jnp.exp(-x))


def _softplus(y):
    return jnp.maximum(y, 0.0) + jnp.log1p(jnp.exp(-jnp.abs(y)))


def _params(sem):
    return pltpu.CompilerParams(dimension_semantics=sem, vmem_limit_bytes=VMEM_LIMIT)


def _lspec(l, tail):
    nz = (0,) * len(tail)
    return pl.BlockSpec((None,) + tuple(tail), lambda *_: (l,) + nz)


def _ffn_body(x_ref, g_ref, wg_ref, wu_ref, wd_ref, o_ref, h_scr, acc_scr, *, nf):
    f = pl.program_id(1)

    @pl.when(f == 0)
    def _():
        h_scr[...] = _rms(x_ref[...], g_ref[...]).astype(BF16)
        acc_scr[...] = jnp.zeros_like(acc_scr)

    h = h_scr[...]
    gate = _dot(h, wg_ref[...])
    up = _dot(h, wu_ref[...])
    act = (gate * _sigmoid(gate) * up).astype(BF16)
    acc_scr[...] += _dot(act, wd_ref[...])

    @pl.when(f == nf - 1)
    def _():
        o_ref[...] = x_ref[...] + 0.5 * acc_scr[...]


def _ffn(x, l, norm, wg, wu, wd):
    n = x.shape[0]
    tm = min(FFN_TM, n)
    nf = FFN_DIM // FFN_TF
    return pl.pallas_call(
        functools.partial(_ffn_body, nf=nf),
        grid=(n // tm, nf),
        in_specs=[
            pl.BlockSpec((tm, D_MODEL), lambda i, f: (i, 0)),
            _lspec(l, (1, D_MODEL)),
            pl.BlockSpec((None, D_MODEL, FFN_TF), lambda i, f: (l, 0, f)),
            pl.BlockSpec((None, D_MODEL, FFN_TF), lambda i, f: (l, 0, f)),
            pl.BlockSpec((None, FFN_TF, D_MODEL), lambda i, f: (l, f, 0)),
        ],
        out_specs=pl.BlockSpec((tm, D_MODEL), lambda i, f: (i, 0)),
        out_shape=jax.ShapeDtypeStruct((n, D_MODEL), F32),
        scratch_shapes=[pltpu.VMEM((tm, D_MODEL), BF16), pltpu.VMEM((tm, D_MODEL), F32)],
        compiler_params=_params(("arbitrary", "arbitrary")),
        name="ffn",
    )(x, norm, wg, wu, wd)


def _head_norm(x, gain, e):
    ms = _dot_exact_rhs(x * x, e) * (1.0 / HEAD_DIM)
    return x * lax.rsqrt(ms + RMS_EPS) * gain


def _mix_body(x_ref, g_ref, w_ref, qg_ref, kg_ref, e_ref, cw_ref, c0_ref,
              h_ref, rkv_ref, q_ref, k_ref, v_ref, ocv_ref, cbuf_ref, tail_scr, *, tm, rs):
    i = pl.program_id(0)
    h = _rms(x_ref[...], g_ref[...])
    h_ref[...] = h
    hb = h.astype(BF16)
    e = e_ref[...]
    o1 = 3 * RW_DIM
    o2 = o1 + SB_DIM
    o3 = o2 + SB_DIM
    o4 = o3 + SB_DIM
    o5 = o4 + CONV_DIM
    o6 = o5 + CONV_DIM
    rkv_ref[...] = _dot(hb, w_ref[:, 0:o1])
    q_ref[...] = _head_norm(_dot(hb, w_ref[:, o1:o2]), qg_ref[...], e) * (HEAD_DIM ** -0.5)
    k_ref[...] = _head_norm(_dot(hb, w_ref[:, o2:o3]), kg_ref[...], e)
    v_ref[...] = _dot(hb, w_ref[:, o3:o4])
    cb = _dot(hb, w_ref[:, o4:o5])
    u = _dot(hb, w_ref[:, o5:o6]) * _dot(hb, w_ref[:, o6:o6 + CONV_DIM])

    w0 = cw_ref[0:1, :]
    w1 = cw_ref[1:2, :]
    w2 = cw_ref[2:3, :]
    seg = min(tm, rs)
    nseg = tm // seg
    if rs >= tm:
        tiles_per_seq = rs // tm

        @pl.when(i % tiles_per_seq == 0)
        def _():
            tail_scr[SUBLANES - 2:SUBLANES, :] = c0_ref[0]

    row = lax.broadcasted_iota(jnp.int32, (seg, CONV_DIM), 0)
    for s in range(nseg):
        us = u[s * seg:(s + 1) * seg]
        if rs >= tm:
            b0 = tail_scr[SUBLANES - 2:SUBLANES - 1, :]
            b1 = tail_scr[SUBLANES - 1:SUBLANES, :]
        else:
            b0 = c0_ref[s, 0:1, :]
            b1 = c0_ref[s, 1:2, :]
        p1 = jnp.where(row == 0, b1, pltpu.roll(us, 1, 0))
        p2 = jnp.where(row == 0, b0, jnp.where(row == 1, b1, pltpu.roll(us, 2, 0)))
        y = p2 * w0 + p1 * w1 + us * w2
        ocv_ref[s * seg:(s + 1) * seg, :] = cb[s * seg:(s + 1) * seg] * y
        tail_scr[...] = us[seg - SUBLANES:seg]
        cbuf_ref[s] = tail_scr[SUBLANES - 2:SUBLANES, :]


def _mix_proj(x, l, rs, norm, w_in, qg, kg, e384, cv_w, conv0):
    n = x.shape[0]
    tm = min(MIX_TM, n)
    nseq = n // rs
    if rs >= tm:
        tps = rs // tm
        nsb = 1
        smap = lambda i: (i // tps, 0, 0)
    else:
        nsb = tm // rs
        smap = lambda i: (i, 0, 0)
    row = lambda i: (i, 0)
    outs = pl.pallas_call(
        functools.partial(_mix_body, tm=tm, rs=rs),
        grid=(n // tm,),
        in_specs=[
            pl.BlockSpec((tm, D_MODEL), row),
            _lspec(l, (1, D_MODEL)),
            _lspec(l, (D_MODEL, 3 * D_MODEL)),
            _lspec(l, (1, SB_DIM)),
            _lspec(l, (1, SB_DIM)),
            pl.BlockSpec((SB_DIM, SB_DIM), lambda i: (0, 0)),
            _lspec(l, (CONV_W, CONV_DIM)),
            pl.BlockSpec((nsb, CONV_W - 1, CONV_DIM), smap),
        ],
        out_specs=[
            pl.BlockSpec((tm, D_MODEL), row),
            pl.BlockSpec((tm, 3 * RW_DIM), row),
            pl.BlockSpec((tm, SB_DIM), row),
            pl.BlockSpec((tm, SB_DIM), row),
            pl.BlockSpec((tm, SB_DIM), row),
            pl.BlockSpec((tm, CONV_DIM), row),
            pl.BlockSpec((nsb, CONV_W - 1, CONV_DIM), smap),
        ],
        out_shape=[
            jax.ShapeDtypeStruct((n, D_MODEL), F32),
            jax.ShapeDtypeStruct((n, 3 * RW_DIM), F32),
            jax.ShapeDtypeStruct((n, SB_DIM), F32),
            jax.ShapeDtypeStruct((n, SB_DIM), F32),
            jax.ShapeDtypeStruct((n, SB_DIM), F32),
            jax.ShapeDtypeStruct((n, CONV_DIM), F32),
            jax.ShapeDtypeStruct((nseq, CONV_W - 1, CONV_DIM), F32),
        ],
        scratch_shapes=[pltpu.VMEM((SUBLANES, CONV_DIM), F32)],
        compiler_params=_params(("arbitrary",)),
        name="mix_proj",
    )(x, norm, w_in, qg, kg, e384, cv_w, conv0)
    return outs


def _state_proj_body(s_ref, w_ref, o_ref):
    o_ref[...] = _dot(s_ref[...].astype(BF16), w_ref[:, 0:3 * RW_DIM])


def _state_proj(shift0, l, w_in):
    b = shift0.shape[0]
    return pl.pallas_call(
        _state_proj_body,
        grid=(1,),
        in_specs=[pl.BlockSpec((b, D_MODEL), lambda i: (0, 0)), _lspec(l, (D_MODEL, 3 * D_MODEL))],
        out_specs=pl.BlockSpec((b, 3 * RW_DIM), lambda i: (0, 0)),
        out_shape=jax.ShapeDtypeStruct((b, 3 * RW_DIM), F32),
        compiler_params=_params(("arbitrary",)),
        name="state_proj",
    )(shift0, w_in)


def _unit_lower_inverse(lmat):
    n = lmat.shape[0]
    eye = (lax.broadcasted_iota(jnp.int32, (n, n), 0) == lax.broadcasted_iota(jnp.int32, (n, n), 1)).astype(F32)
    p = eye + lmat
    lb = lmat.astype(BF16)
    lk = _dot(lb, lb)
    power = 2
    while power < n:
        lkb = lk.astype(BF16)
        if 2 * power < n:
            both = _dot(lkb, jnp.concatenate([lk, p], axis=1).astype(BF16))
            lk = both[:, 0:n]
            p = p + both[:, n:2 * n]
        else:
            p = p + _dot(lkb, p.astype(BF16))
        power *= 2
    return p


def _rwkv_body(h_ref, rkv_ref, sh0_ref, rp0_ref, s0_ref,
               murkv_ref, muwag_ref, w0_ref, w1_ref, w2_ref, a0_ref, a1_ref, a2_ref, g1_ref, g2_ref,
               kk_ref, ka_ref, rk_ref, lnw_ref, lnb_ref, e_ref, tri_ref,
               o_ref, sout_ref, hl_scr, rl_scr, s_scr, *, tm, rs):
    i = pl.program_id(0)
    seg = min(tm, rs)
    nseg = tm // seg
    carried = rs >= tm
    if carried:
        tiles_per_seq = rs // tm

        @pl.when(i % tiles_per_seq == 0)
        def _():
            hl_scr[SUBLANES - 1:SUBLANES, :] = sh0_ref[0]
            rl_scr[SUBLANES - 1:SUBLANES, :] = rp0_ref[0]
            s_scr[...] = s0_ref[0]

    h = h_ref[...]
    rkv = rkv_ref[...]
    if carried:
        hprev_rows = hl_scr[SUBLANES - 1:SUBLANES, :]
        rprev_rows = rl_scr[SUBLANES - 1:SUBLANES, :]
    else:
        hprev_rows = jnp.concatenate(
            [jnp.broadcast_to(sh0_ref[s], (seg, D_MODEL)) for s in range(nseg)], axis=0)
        rprev_rows = jnp.concatenate(
            [jnp.broadcast_to(rp0_ref[s], (seg, 3 * RW_DIM)) for s in range(nseg)], axis=0)
    first_h = (lax.broadcasted_iota(jnp.int32, (tm, D_MODEL), 0) & (seg - 1)) == 0
    first_r = (lax.broadcasted_iota(jnp.int32, (tm, 3 * RW_DIM), 0) & (seg - 1)) == 0
    hp = jnp.where(first_h, hprev_rows, pltpu.roll(h, 1, 0))
    rp = jnp.where(first_r, rprev_rows, pltpu.roll(rkv, 1, 0))
    if carried:
        hl_scr[...] = h[tm - SUBLANES:tm]
        rl_scr[...] = rkv[tm - SUBLANES:tm]

    rkv = rkv + murkv_ref[...] * (rp - rkv)
    r = rkv[:, 0:RW_DIM]
    k = rkv[:, RW_DIM:2 * RW_DIM]
    v = rkv[:, 2 * RW_DIM:3 * RW_DIM]
    dx = hp - h
    xw = (h + muwag_ref[0:1, :] * dx).astype(BF16)
    xa = (h + muwag_ref[1:2, :] * dx).astype(BF16)
    xg = (h + muwag_ref[2:3, :] * dx).astype(BF16)
    wl = w0_ref[...] + _dot(jnp.tanh(_dot(xw, w1_ref[...])).astype(BF16), w2_ref[...])
    w = -_softplus(-wl) - 0.5
    logdecay = -jnp.exp(w)
    a = _sigmoid(a0_ref[...] + _dot(_dot(xa, a1_ref[...]).astype(BF16), a2_ref[...]))
    gate = _dot(_sigmoid(_dot(xg, g1_ref[...])).astype(BF16), g2_ref[...])
    e = e_ref[...]
    kk = k * kk_ref[...]
    kk = kk * lax.rsqrt(_dot_exact_rhs(kk * kk, e) + L2_EPS)
    k = k * (1.0 + (a - 1.0) * ka_ref[...])
    b = kk * a
    bonus = _dot_exact_rhs(r * k * rk_ref[...], e) * v

    tri = tri_ref[...]
    ci = lax.broadcasted_iota(jnp.int32, (CHUNK, CHUNK), 0)
    cj = lax.broadcasted_iota(jnp.int32, (CHUNK, CHUNK), 1)
    strict = cj < ci
    incl = cj <= ci

    if carried:
        state = [s_scr[hh] for hh in range(RW_HEADS)]
    for c in range(tm // CHUNK):
        sl = slice(c * CHUNK, (c + 1) * CHUNK)
        sq = (c * CHUNK) // seg
        if not carried:
            state = [s0_ref[sq, hh] for hh in range(RW_HEADS)]
        ld = logdecay[sl]
        cum = _dot_exact_lhs(tri, ld)
        tot = cum[CHUNK - 1:CHUNK, :]
        g_incl = jnp.exp(cum)
        g_prev = jnp.exp(cum - ld)
        g_inv = jnp.exp(-cum)
        g_rem = jnp.exp(tot - cum)
        g_tot = jnp.exp(tot)
        rc, kc, vc, kkc, bc = r[sl], k[sl], v[sl], kk[sl], b[sl]
        rq = rc * g_incl
        kq = kc * g_inv
        bq = bc * g_inv
        aq = -kkc * g_prev
        kz = kc * g_rem
        bz = bc * g_rem
        outs = []
        for hh in range(RW_HEADS):
            hs = slice(hh * HEAD_DIM, (hh + 1) * HEAD_DIM)
            rq_h = rq[:, hs].astype(BF16)
            kq_h = kq[:, hs].astype(BF16)
            bq_h = bq[:, hs].astype(BF16)
            aq_h = aq[:, hs].astype(BF16)
            kz_h = kz[:, hs].astype(BF16)
            bz_h = bz[:, hs].astype(BF16)
            v_h = vc[:, hs].astype(BF16)
            m_ak = jnp.where(strict, _dot_nt(aq_h, kq_h), 0.0)
            m_ab = jnp.where(strict, _dot_nt(aq_h, bq_h), 0.0)
            m_rk = jnp.where(incl, _dot_nt(rq_h, kq_h), 0.0)
            m_rb = jnp.where(incl, _dot_nt(rq_h, bq_h), 0.0)
            tinv = _unit_lower_inverse(m_ab).astype(BF16)
            s_h = state[hh]
            s_b = s_h.astype(BF16)
            rhs = _dot_nt(aq_h, s_b) + _dot(m_ak.astype(BF16), v_h)
            u = _dot(tinv, rhs.astype(BF16))
            u_b = u.astype(BF16)
            o_h = _dot_nt(rq_h, s_b) + _dot(m_rk.astype(BF16), v_h) + _dot(m_rb.astype(BF16), u_b)
            state[hh] = s_h * g_tot[:, hs] + _dot_tn(v_h, kz_h) + _dot_tn(u_b, bz_h)
            outs.append(o_h)
            if not carried:
                sout_ref[sq, hh] = state[hh]
        o = jnp.concatenate(outs, axis=1)
        mean = _dot_exact_rhs(o, e) * (1.0 / HEAD_DIM)
        d = o - mean
        var = _dot_exact_rhs(d * d, e) * (1.0 / HEAD_DIM)
        on = d * lax.rsqrt(var + GN_EPS) * lnw_ref[...] + lnb_ref[...]
        o_ref[sl, :] = (on + bonus[sl]) * gate[sl]
    if carried:
        for hh in range(RW_HEADS):
            s_scr[hh] = state[hh]
            sout_ref[0, hh] = state[hh]


def _rwkv(h, rkv, l, rs, shift0, rkvprev0, s0, pr, e384, tri):
    n = h.shape[0]
    tm = min(RWKV_TM, n)
    nseq = n // rs
    if rs >= tm:
        tps = rs // tm
        nsb = 1
        smap3 = lambda i: (i // tps, 0, 0)
        smap4 = lambda i: (i // tps, 0, 0, 0)
    else:
        nsb = tm // rs
        smap3 = lambda i: (i, 0, 0)
        smap4 = lambda i: (i, 0, 0, 0)
    row = lambda i: (i, 0)
    lora_w, lora_a, lora_g = pr["rw_w1"].shape[-1], pr["rw_a1"].shape[-1], pr["rw_g1"].shape[-1]
    return pl.pallas_call(
        functools.partial(_rwkv_body, tm=tm, rs=rs),
        grid=(n // tm,),
        in_specs=[
            pl.BlockSpec((tm, D_MODEL), row),
            pl.BlockSpec((tm, 3 * RW_DIM), row),
            pl.BlockSpec((nsb, 1, D_MODEL), smap3),
            pl.BlockSpec((nsb, 1, 3 * RW_DIM), smap3),
            pl.BlockSpec((nsb, RW_HEADS, HEAD_DIM, HEAD_DIM), smap4),
            _lspec(l, (1, 3 * RW_DIM)),
            _lspec(l, (3, D_MODEL)),
            _lspec(l, (1, RW_DIM)),
            _lspec(l, (D_MODEL, lora_w)),
            _lspec(l, (lora_w, RW_DIM)),
            _lspec(l, (1, RW_DIM)),
            _lspec(l, (D_MODEL, lora_a)),
            _lspec(l, (lora_a, RW_DIM)),
            _lspec(l, (D_MODEL, lora_g)),
            _lspec(l, (lora_g, RW_DIM)),
            _lspec(l, (1, RW_DIM)),
            _lspec(l, (1, RW_DIM)),
            _lspec(l, (1, RW_DIM)),
            _lspec(l, (1, RW_DIM)),
            _lspec(l, (1, RW_DIM)),
            pl.BlockSpec((RW_DIM, RW_DIM), lambda i: (0, 0)),
            pl.BlockSpec((CHUNK, CHUNK), lambda i: (0, 0)),
        ],
        out_specs=[
            pl.BlockSpec((tm, RW_DIM), row),
            pl.BlockSpec((nsb, RW_HEADS, HEAD_DIM, HEAD_DIM), smap4),
        ],
        out_shape=[
            jax.ShapeDtypeStruct((n, RW_DIM), F32),
            jax.ShapeDtypeStruct((nseq, RW_HEADS, HEAD_DIM, HEAD_DIM), F32),
        ],
        scratch_shapes=[
            pltpu.VMEM((SUBLANES, D_MODEL), F32),
            pltpu.VMEM((SUBLANES, 3 * RW_DIM), F32),
            pltpu.VMEM((RW_HEADS, HEAD_DIM, HEAD_DIM), F32),
        ],
        compiler_params=_params(("arbitrary",)),
        name="rwkv",
    )(h, rkv, shift0, rkvprev0, s0,
      pr["rw_mu_rkv"], pr["rw_mu_wag"], pr["rw_w0"], pr["rw_w1"], pr["rw_w2"], pr["rw_a0"], pr["rw_a1"],
      pr["rw_a2"], pr["rw_g1"], pr["rw_g2"], pr["rw_k_k"], pr["rw_k_a"], pr["rw_r_k"], pr["rw_ln_w"],
      pr["rw_ln_b"], e384, tri)


def _sb_block(qm, kb, vb, lower, carry, acc, mask):
    z = _dot_nt(qm, kb)
    sp = jnp.log1p(jnp.exp(-jnp.abs(z)))
    log_beta = jnp.minimum(z, 0.0) - sp
    log_rest = jnp.minimum(-z, 0.0) - sp
    if mask is not None:
        log_rest = jnp.where(mask, log_rest, 0.0)
    tail = _dot_exact_rhs(log_rest, lower)
    wts = jnp.exp(log_beta + tail + carry)
    if mask is not None:
        wts = jnp.where(mask, wts, 0.0)
    acc = acc + _dot(wts.astype(BF16), vb)
    carry = carry + tail[:, 0:1] + log_rest[:, 0:1]
    return carry, acc


def _sb_body(*refs, tq, n_past, tkp):
    if n_past:
        q_ref, kc_ref, vc_ref, kp_ref, vp_ref, lowc_ref, lowp_ref, o_ref = refs
    else:
        q_ref, kc_ref, vc_ref, lowc_ref, o_ref = refs
    qi = pl.program_id(1)
    lane_lo = lax.broadcasted_iota(jnp.int32, (1, LANES), 1) < HEAD_DIM
    qpos = lax.broadcasted_iota(jnp.int32, (tq, tq), 0)
    kpos = lax.broadcasted_iota(jnp.int32, (tq, tq), 1)
    diag_mask = kpos < qpos
    lowc = lowc_ref[...]
    for p in range(SB_DIM // LANES):
        cs = slice(p * LANES, (p + 1) * LANES)
        qp = q_ref[:, cs]
        qms = [jnp.where(lane_lo, qp, 0.0).astype(BF16), jnp.where(lane_lo, 0.0, qp).astype(BF16)]
        start = pl.multiple_of(qi * tq, tq)
        kb = kc_ref[pl.ds(start, tq), cs].astype(BF16)
        vb = vc_ref[pl.ds(start, tq), cs].astype(BF16)
        st = []
        for half in range(2):
            st.extend(_sb_block(qms[half], kb, vb, lowc, jnp.zeros((tq, 1), F32),
                                jnp.zeros((tq, LANES), F32), diag_mask))

        def cur_step(jj, st):
            j = qi - 1 - jj
            off = pl.multiple_of(j * tq, tq)
            kb = kc_ref[pl.ds(off, tq), cs].astype(BF16)
            vb = vc_ref[pl.ds(off, tq), cs].astype(BF16)
            c0, a0 = _sb_block(qms[0], kb, vb, lowc, st[0], st[1], None)
            c1, a1 = _sb_block(qms[1], kb, vb, lowc, st[2], st[3], None)
            return (c0, a0, c1, a1)

        st = lax.fori_loop(0, qi, cur_step, tuple(st))
        if n_past:
            lowp = lowp_ref[...]

            def past_step(jj, st):
                j = n_past - 1 - jj
                off = pl.multiple_of(j * tkp, tkp)
                kb = kp_ref[pl.ds(off, tkp), cs].astype(BF16)
                vb = vp_ref[pl.ds(off, tkp), cs].astype(BF16)
                c0, a0 = _sb_block(qms[0], kb, vb, lowp, st[0], st[1], None)
                c1, a1 = _sb_block(qms[1], kb, vb, lowp, st[2], st[3], None)
                return (c0, a0, c1, a1)

            st = lax.fori_loop(0, n_past, past_step, st)
        o_ref[:, cs] = jnp.where(lane_lo, st[1], st[3])


def _stick_breaking(q, k, v, l, nseq, past_k, past_v, lowc, lowp):
    n = q.shape[0]
    t = n // nseq
    tq = min(SB_TQ, t)
    nq = t // tq
    qmap = lambda b, i: (b * nq + i, 0)
    smap = lambda b, i: (b, 0)
    in_specs = [
        pl.BlockSpec((tq, SB_DIM), qmap),
        pl.BlockSpec((t, SB_DIM), smap),
        pl.BlockSpec((t, SB_DIM), smap),
    ]
    args = [q, k, v]
    n_past = 0
    if past_k is not None:
        plen = past_k.shape[2]
        n_past = plen // SB_TKP
        pmap = lambda b, i: (l, b, 0, 0)
        in_specs += [pl.BlockSpec((None, None, plen, SB_DIM), pmap), pl.BlockSpec((None, None, plen, SB_DIM), pmap)]
        args += [past_k, past_v]
    in_specs.append(pl.BlockSpec((tq, tq), lambda b, i: (0, 0)))
    args.append(lowc)
    if n_past:
        in_specs.append(pl.BlockSpec((SB_TKP, SB_TKP), lambda b, i: (0, 0)))
        args.append(lowp)
    return pl.pallas_call(
        functools.partial(_sb_body, tq=tq, n_past=n_past, tkp=SB_TKP),
        grid=(nseq, nq),
        in_specs=in_specs,
        out_specs=pl.BlockSpec((tq, SB_DIM), qmap),
        out_shape=jax.ShapeDtypeStruct((n, SB_DIM), F32),
        compiler_params=_params(("arbitrary", "arbitrary")),
        name="stick_breaking",
    )(*args)


def _xattn_body(x_ref, orw_ref, osb_ref, ocv_ref, wout_ref, g_ref, wq_ref, qg_ref, mk_ref, mv_ref, wo_ref, o_ref):
    x = x_ref[...]
    x = x + _dot(orw_ref[...].astype(BF16), wout_ref[0:RW_DIM, :])
    x = x + _dot(osb_ref[...].astype(BF16), wout_ref[RW_DIM:RW_DIM + SB_DIM, :])
    x = x + _dot(ocv_ref[...].astype(BF16), wout_ref[RW_DIM + SB_DIM:D_MODEL, :])
    hx = _rms(x, g_ref[...]).astype(BF16)
    qf = _dot(hx, wq_ref[...])
    heads = []
    for hh in range(X_HEADS):
        cs = slice(hh * X_HEAD_DIM, (hh + 1) * X_HEAD_DIM)
        qh = _rms(qf[:, cs], qg_ref[...]).astype(BF16)
        s = _dot_nt(qh, mk_ref[:, cs].astype(BF16)) * (X_HEAD_DIM ** -0.5)
        s = jnp.exp(s - jnp.max(s, axis=-1, keepdims=True))
        attn = s / jnp.sum(s, axis=-1, keepdims=True)
        heads.append(_dot(attn.astype(BF16), mv_ref[:, cs].astype(BF16)))
    o = jnp.concatenate(heads, axis=1).astype(BF16)
    o_ref[...] = x + _dot(o, wo_ref[...])


def _xattn(x, orw, osb, ocv, l, nseq, w_out, x_norm, wq, q_norm, mem_k, mem_v, mem_l, wo):
    n = x.shape[0]
    t = n // nseq
    tq = min(XA_TQ, t)
    nq = t // tq
    nm = mem_k.shape[2]
    row = lambda b, i: (b * nq + i, 0)
    mmap = lambda b, i: (mem_l, b, 0, 0)
    return pl.pallas_call(
        _xattn_body,
        grid=(nseq, nq),
        in_specs=[
            pl.BlockSpec((tq, D_MODEL), row),
            pl.BlockSpec((tq, RW_DIM), row),
            pl.BlockSpec((tq, SB_DIM), row),
            pl.BlockSpec((tq, CONV_DIM), row),
            _lspec(l, (D_MODEL, D_MODEL)),
            _lspec(l, (1, D_MODEL)),
            _lspec(l, (D_MODEL, D_MODEL)),
            _lspec(l, (1, X_HEAD_DIM)),
            pl.BlockSpec((None, None, nm, D_MODEL), mmap),
            pl.BlockSpec((None, None, nm, D_MODEL), mmap),
            _lspec(l, (D_MODEL, D_MODEL)),
        ],
        out_specs=pl.BlockSpec((tq, D_MODEL), row),
        out_shape=jax.ShapeDtypeStruct((n, D_MODEL), F32),
        compiler_params=_params(("arbitrary", "arbitrary")),
        name="xattn",
    )(x, orw, osb, ocv, w_out, x_norm, wq, q_norm, mem_k, mem_v, wo)


def _memkv_body(m_ref, g_ref, wk_ref, wv_ref, kg_ref, mk_ref, mv_ref):
    m = _rms(m_ref[...], g_ref[...]).astype(BF16)
    kf = _dot(m, wk_ref[...])
    for hh in range(X_HEADS):
        cs = slice(hh * X_HEAD_DIM, (hh + 1) * X_HEAD_DIM)
        mk_ref[:, cs] = _rms(kf[:, cs], kg_ref[...])
    mv_ref[...] = _dot(m, wv_ref[...])


def _memory_kv(mem, l, mem_norm, wk, wv, k_norm):
    n = mem.shape[0]
    tm = min(MEM_TM, n)
    row = lambda i: (i, 0)
    return pl.pallas_call(
        _memkv_body,
        grid=(n // tm,),
        in_specs=[
            pl.BlockSpec((tm, D_MODEL), row),
            _lspec(l, (1, D_MODEL)),
            _lspec(l, (D_MODEL, D_MODEL)),
            _lspec(l, (D_MODEL, D_MODEL)),
            _lspec(l, (1, X_HEAD_DIM)),
        ],
        out_specs=[pl.BlockSpec((tm, D_MODEL), row), pl.BlockSpec((tm, D_MODEL), row)],
        out_shape=[jax.ShapeDtypeStruct((n, D_MODEL), F32), jax.ShapeDtypeStruct((n, D_MODEL), F32)],
        compiler_params=_params(("arbitrary",)),
        name="memory_kv",
    )(mem, mem_norm, wk, wv, k_norm)


def _layer(x, l, nseq, mem_k, mem_v, mem_l, past_k, past_v, s0, shift0, rkvprev0, conv0, P, C):
    n = x.shape[0]
    rs = n // nseq
    x = _ffn(x, l, P["ffn1_norm"], P["ffn1_wg"], P["ffn1_wu"], P["ffn1_wd"])
    h, rkv, q, k_new, v_new, o_cv, conv_buf = _mix_proj(
        x, l, rs, P["mix_norm"], P["w_in"], P["sb_q_norm"], P["sb_k_norm"], C["e384"], P["cv_w"], conv0)
    o_rw, s_t = _rwkv(h, rkv, l, rs, shift0, rkvprev0, s0, P, C["e384"], C["tri"])
    tq = min(SB_TQ, rs)
    o_sb = _stick_breaking(q, k_new, v_new, l, nseq, past_k, past_v, C["low%d" % tq], C["low%d" % SB_TKP])
    x = _xattn(x, o_rw, o_sb, o_cv, l, nseq, P["w_out"], P["x_norm"], P["x_wq"], P["x_q_norm"],
               mem_k, mem_v, mem_l, P["x_wo"])
    x = _ffn(x, l, P["ffn2_norm"], P["ffn2_wg"], P["ffn2_wu"], P["ffn2_wd"])
    h_last = h.reshape(nseq, rs, D_MODEL)[:, rs - 1]
    return x, k_new, v_new, s_t, h_last, conv_buf


def _lower_ones(n, strict):
    i = lax.broadcasted_iota(jnp.int32, (n, n), 0)
    j = lax.broadcasted_iota(jnp.int32, (n, n), 1)
    return ((j < i) if strict else (j <= i)).astype(BF16)


def kernel(x_prompt, x_sample, mem_prompt, cache_sb_k, cache_sb_v, state_rwkv, state_shift, state_conv,
           cache_mem_k, cache_mem_v, ffn1_norm, ffn1_wg, ffn1_wu, ffn1_wd, mix_norm, w_in, w_out, rw_mu_rkv,
           rw_mu_wag, rw_w0, rw_w1, rw_w2, rw_a0, rw_a1, rw_a2, rw_g1, rw_g2, rw_k_k, rw_k_a, rw_r_k, rw_ln_w,
           rw_ln_b, sb_q_norm, sb_k_norm, cv_w, x_norm, mem_norm, x_wq, x_wk, x_wv, x_wo, x_q_norm, x_k_norm,
           ffn2_norm, ffn2_wg, ffn2_wu, ffn2_wd):
    depth = w_in.shape[0]
    bp, tp, _ = x_prompt.shape
    bs, ts, _ = x_sample.shape
    n_mem = mem_prompt.shape[1]
    plen = cache_sb_k.shape[2]

    def vec(a):
        return a.reshape(a.shape[0], 1, -1)

    def per_head(a):
        return jnp.tile(a, (1, SB_HEADS)).reshape(a.shape[0], 1, SB_DIM)

    P = dict(
        ffn1_norm=vec(ffn1_norm), ffn1_wg=ffn1_wg.astype(BF16), ffn1_wu=ffn1_wu.astype(BF16),
        ffn1_wd=ffn1_wd.astype(BF16),
        mix_norm=vec(mix_norm), w_in=w_in.astype(BF16), w_out=w_out.astype(BF16),
        rw_mu_rkv=rw_mu_rkv.reshape(depth, 1, 3 * RW_DIM), rw_mu_wag=rw_mu_wag,
        rw_w0=vec(rw_w0), rw_w1=rw_w1.astype(BF16), rw_w2=rw_w2.astype(BF16),
        rw_a0=vec(rw_a0), rw_a1=rw_a1.astype(BF16), rw_a2=rw_a2.astype(BF16),
        rw_g1=rw_g1.astype(BF16), rw_g2=rw_g2.astype(BF16),
        rw_k_k=vec(rw_k_k), rw_k_a=vec(rw_k_a), rw_r_k=rw_r_k.reshape(depth, 1, RW_DIM),
        rw_ln_w=vec(rw_ln_w), rw_ln_b=vec(rw_ln_b),
        sb_q_norm=per_head(sb_q_norm), sb_k_norm=per_head(sb_k_norm), cv_w=cv_w,
        x_norm=vec(x_norm), mem_norm=vec(mem_norm),
        x_wq=x_wq.astype(BF16), x_wk=x_wk.astype(BF16), x_wv=x_wv.astype(BF16), x_wo=x_wo.astype(BF16),
        x_q_norm=vec(x_q_norm), x_k_norm=vec(x_k_norm),
        ffn2_norm=vec(ffn2_norm), ffn2_wg=ffn2_wg.astype(BF16), ffn2_wu=ffn2_wu.astype(BF16),
        ffn2_wd=ffn2_wd.astype(BF16),
    )
    hid = lax.broadcasted_iota(jnp.int32, (RW_DIM, RW_DIM), 0) // HEAD_DIM
    hjd = lax.broadcasted_iota(jnp.int32, (RW_DIM, RW_DIM), 1) // HEAD_DIM
    C = {"e384": (hid == hjd).astype(BF16), "tri": _lower_ones(CHUNK, False)}
    for size in {min(SB_TQ, tp), min(SB_TQ, ts), SB_TKP}:
        C["low%d" % size] = _lower_ones(size, True)

    dt = x_prompt.dtype
    zero_s = jnp.zeros((bp, RW_HEADS, HEAD_DIM, HEAD_DIM), state_rwkv.dtype)
    zero_shift = jnp.zeros((bp, 1, D_MODEL), dt)
    zero_rkv = jnp.zeros((bp, 1, 3 * RW_DIM), dt)
    zero_conv = jnp.zeros((bp, CONV_W - 1, CONV_DIM), dt)
    past_k = cache_sb_k.reshape(depth, bs, plen, SB_DIM)
    past_v = cache_sb_v.reshape(depth, bs, plen, SB_DIM)
    cmk = cache_mem_k.reshape(depth, bs, n_mem, D_MODEL)
    cmv = cache_mem_v.reshape(depth, bs, n_mem, D_MODEL)

    yp = x_prompt.reshape(bp * tp, D_MODEL)
    ys = x_sample.reshape(bs * ts, D_MODEL)
    mem = mem_prompt.reshape(bp * n_mem, D_MODEL)
    pk, pv, pS, psh, pcv, pmk, pmv = [], [], [], [], [], [], []
    sk, sv, sS, ssh, scv = [], [], [], [], []
    for l in range(depth):
        mk, mv = _memory_kv(mem, l, P["mem_norm"], P["x_wk"], P["x_wv"], P["x_k_norm"])
        mk4 = mk.reshape(1, bp, n_mem, D_MODEL)
        mv4 = mv.reshape(1, bp, n_mem, D_MODEL)
        yp, kn, vn, s_t, sh, cbuf = _layer(yp, l, bp, mk4, mv4, 0, None, None, zero_s, zero_shift, zero_rkv,
                                           zero_conv, P, C)
        pk.append(kn); pv.append(vn); pS.append(s_t); psh.append(sh); pcv.append(cbuf)
        pmk.append(mk); pmv.append(mv)
        shift_l = state_shift[l]
        rkvprev = _state_proj(shift_l, l, P["w_in"]).reshape(bs, 1, 3 * RW_DIM)
        ys, kn, vn, s_t, sh, cbuf = _layer(ys, l, bs, cmk, cmv, l, past_k, past_v, state_rwkv[l],
                                           shift_l.reshape(bs, 1, D_MODEL), rkvprev, state_conv[l], P, C)
        sk.append(kn); sv.append(vn); sS.append(s_t); ssh.append(sh); scv.append(cbuf)

    def stack(xs, shape):
        return jnp.stack(xs).reshape((depth,) + shape)

    return (
        yp.reshape(bp, tp, D_MODEL), ys.reshape(bs, ts, D_MODEL),
        stack(pk, (bp, tp, SB_HEADS, HEAD_DIM)), stack(pv, (bp, tp, SB_HEADS, HEAD_DIM)),
        stack(pS, (bp, RW_HEADS, HEAD_DIM, HEAD_DIM)), stack(psh, (bp, D_MODEL)),
        stack(pcv, (bp, CONV_W - 1, CONV_DIM)),
        stack(pmk, (bp, n_mem, X_HEADS, X_HEAD_DIM)), stack(pmv, (bp, n_mem, X_HEADS, X_HEAD_DIM)),
        stack(sk, (bs, ts, SB_HEADS, HEAD_DIM)), stack(sv, (bs, ts, SB_HEADS, HEAD_DIM)),
        stack(sS, (bs, RW_HEADS, HEAD_DIM, HEAD_DIM)), stack(ssh, (bs, D_MODEL)),
        stack(scv, (bs, CONV_W - 1, CONV_DIM)),
    )
```

```python
import functools

import jax
import jax.numpy as jnp
from jax import lax
from jax.experimental import pallas as pl
from jax.experimental.pallas import tpu as pltpu

F32 = jnp.float32
BF16 = jnp.bfloat16

D_MODEL = 1024
HEAD_DIM = 64
RW_HEADS = 6
SB_HEADS = 6
RW_DIM = RW_HEADS * HEAD_DIM
SB_DIM = SB_HEADS * HEAD_DIM
CONV_DIM = 256
CONV_W = 3
FFN_DIM = 2816
X_HEADS = 4
X_HEAD_DIM = D_MODEL // X_HEADS
RMS_EPS = 1e-6
GN_EPS = 64e-5
L2_EPS = 1e-12

CHUNK = 64
LANES = 128
SUBLANES = 8
PAIRS = RW_DIM // LANES
VMEM_LIMIT = 56 * 1024 * 1024

FFN_TM = 1024
FFN_TF = 256
MIX_TM = 512
RWKV_TM = 256
SB_TQ = 128
SB_TK = 256
XA_TQ = 512
MEM_TM = 512

_RKV = 3 * RW_DIM
_Q0 = _RKV
_K0 = _Q0 + SB_DIM
_V0 = _K0 + SB_DIM
_C0 = _V0 + SB_DIM


def _dot(a, b):
    return jnp.dot(a, b, preferred_element_type=F32)


def _dot_nt(a, b):
    return lax.dot_general(a, b, (((1,), (1,)), ((), ())), preferred_element_type=F32)


def _dot_tn(a, b):
    return lax.dot_general(a, b, (((0,), (0,)), ((), ())), preferred_element_type=F32)


def _split(x, terms):
    parts = []
    rem = x
    for t in range(terms):
        p = rem.astype(BF16)
        parts.append(p)
        if t + 1 < terms:
            rem = rem - p.astype(F32)
    return parts


def _dot_exact_rhs(x, m, terms=3):
    acc = None
    for p in _split(x, terms):
        y = _dot(p, m)
        acc = y if acc is None else acc + y
    return acc


def _dot_exact_lhs(m, x, terms=3):
    acc = None
    for p in _split(x, terms):
        y = _dot(m, p)
        acc = y if acc is None else acc + y
    return acc


def _rms(x, g):
    return x * lax.rsqrt(jnp.mean(x * x, axis=-1, keepdims=True) + RMS_EPS) * g


def _sigmoid(x):
    return 1.0 / (1.0 + jnp.exp(-x))


def _softplus(y):
    return jnp.maximum(y, 0.0) + jnp.log1p(jnp.exp(-jnp.abs(y)))


def _params(sem):
    return pltpu.CompilerParams(dimension_semantics=sem, vmem_limit_bytes=VMEM_LIMIT)


def _lspec(l, tail):
    nz = (0,) * len(tail)
    return pl.BlockSpec((None,) + tuple(tail), lambda *_: (l,) + nz)


def _lane_lo():
    return lax.broadcasted_iota(jnp.int32, (1, LANES), 1) < HEAD_DIM


def _pair_stack(xp, lane_lo):
    return jnp.concatenate([jnp.where(lane_lo, xp, 0.0), jnp.where(lane_lo, 0.0, xp)], axis=0)


def _ffn_body(x_ref, g_ref, wg_ref, wu_ref, wd_ref, o_ref, h_scr, acc_scr, *, nf):
    f = pl.program_id(1)

    @pl.when(f == 0)
    def _():
        h_scr[...] = _rms(x_ref[...], g_ref[...]).astype(BF16)
        acc_scr[...] = jnp.zeros_like(acc_scr)

    h = h_scr[...]
    gate = _dot(h, wg_ref[...])
    up = _dot(h, wu_ref[...])
    act = (gate * _sigmoid(gate) * up).astype(BF16)
    acc_scr[...] += _dot(act, wd_ref[...])

    @pl.when(f == nf - 1)
    def _():
        o_ref[...] = x_ref[...] + 0.5 * acc_scr[...]


def _ffn(x, l, norm, wg, wu, wd):
    n = x.shape[0]
    tm = min(FFN_TM, n)
    nf = FFN_DIM // FFN_TF
    return pl.pallas_call(
        functools.partial(_ffn_body, nf=nf),
        grid=(n // tm, nf),
        in_specs=[
            pl.BlockSpec((tm, D_MODEL), lambda i, f: (i, 0)),
            _lspec(l, (1, D_MODEL)),
            pl.BlockSpec((None, D_MODEL, FFN_TF), lambda i, f: (l, 0, f)),
            pl.BlockSpec((None, D_MODEL, FFN_TF), lambda i, f: (l, 0, f)),
            pl.BlockSpec((None, FFN_TF, D_MODEL), lambda i, f: (l, f, 0)),
        ],
        out_specs=pl.BlockSpec((tm, D_MODEL), lambda i, f: (i, 0)),
        out_shape=jax.ShapeDtypeStruct((n, D_MODEL), F32),
        scratch_shapes=[pltpu.VMEM((tm, D_MODEL), BF16), pltpu.VMEM((tm, D_MODEL), F32)],
        compiler_params=_params(("arbitrary", "arbitrary")),
        name="ffn",
    )(x, norm, wg, wu, wd)


def _mix_body(x_ref, g_ref, wrq_ref, wkv_ref, wcv_ref, qg_ref, kg_ref, e_ref, cw_ref, c0_ref, kin_ref, vin_ref,
              h_ref, rkv_ref, q_ref, ocv_ref, cbuf_ref, kt_ref, vt_ref, tail_scr, *, tm, rs):
    del kin_ref, vin_ref
    i = pl.program_id(0)
    h = _rms(x_ref[...], g_ref[...])
    h_ref[...] = h
    hb = h.astype(BF16)
    rkv_ref[...] = _dot(hb, wrq_ref[:, 0:_RKV])
    sq = _dot(hb, wrq_ref[:, _RKV:_RKV + SB_DIM])
    ms = _dot_exact_rhs(sq * sq, e_ref[...]) * (1.0 / HEAD_DIM)
    q_ref[...] = sq * lax.rsqrt(ms + RMS_EPS) * qg_ref[...] * (HEAD_DIM ** -0.5)

    kt = _dot_nt(wkv_ref[0:SB_DIM, :], hb)
    k3 = kt.reshape(SB_HEADS, HEAD_DIM, tm)
    kms = jnp.mean(k3 * k3, axis=1, keepdims=True)
    kt_ref[...] = (k3 * lax.rsqrt(kms + RMS_EPS)).reshape(SB_DIM, tm) * kg_ref[...]
    vt_ref[...] = _dot_nt(wkv_ref[SB_DIM:2 * SB_DIM, :], hb)

    cb = _dot(hb, wcv_ref[:, 0:CONV_DIM])
    u = _dot(hb, wcv_ref[:, CONV_DIM:2 * CONV_DIM]) * _dot(hb, wcv_ref[:, 2 * CONV_DIM:3 * CONV_DIM])
    w0 = cw_ref[0:1, :]
    w1 = cw_ref[1:2, :]
    w2 = cw_ref[2:3, :]
    seg = min(tm, rs)
    nseg = tm // seg
    if rs >= tm:
        tiles_per_seq = rs // tm

        @pl.when(i % tiles_per_seq == 0)
        def _():
            tail_scr[SUBLANES - 2:SUBLANES, :] = c0_ref[0]

    row = lax.broadcasted_iota(jnp.int32, (seg, CONV_DIM), 0)
    for s in range(nseg):
        us = u[s * seg:(s + 1) * seg]
        if rs >= tm:
            b0 = tail_scr[SUBLANES - 2:SUBLANES - 1, :]
            b1 = tail_scr[SUBLANES - 1:SUBLANES, :]
        else:
            b0 = c0_ref[s, 0:1, :]
            b1 = c0_ref[s, 1:2, :]
        p1 = jnp.where(row == 0, b1, pltpu.roll(us, 1, 0))
        p2 = jnp.where(row == 0, b0, jnp.where(row == 1, b1, pltpu.roll(us, 2, 0)))
        y = p2 * w0 + p1 * w1 + us * w2
        ocv_ref[s * seg:(s + 1) * seg, :] = cb[s * seg:(s + 1) * seg] * y
        tail_scr[...] = us[seg - SUBLANES:seg]
        cbuf_ref[s] = tail_scr[SUBLANES - 2:SUBLANES, :]


def _mix_proj(x, l, rs, P, C, conv0, kt_all, vt_all):
    n = x.shape[0]
    tm = min(MIX_TM, n)
    nseq = n // rs
    if rs >= tm:
        tps = rs // tm
        nsb = 1
        smap = lambda i: (i // tps, 0, 0)
        tmap = lambda i: (l, i // tps, 0, i % tps)
    else:
        nsb = tm // rs
        smap = lambda i: (i, 0, 0)
        tmap = lambda i: (l, 0, 0, i)
    row = lambda i: (i, 0)
    tspec = pl.BlockSpec((None, None, SB_DIM, tm), tmap)
    return pl.pallas_call(
        functools.partial(_mix_body, tm=tm, rs=rs),
        grid=(n // tm,),
        in_specs=[
            pl.BlockSpec((tm, D_MODEL), row),
            _lspec(l, (1, D_MODEL)),
            _lspec(l, (D_MODEL, _RKV + SB_DIM)),
            _lspec(l, (2 * SB_DIM, D_MODEL)),
            _lspec(l, (D_MODEL, 3 * CONV_DIM)),
            _lspec(l, (1, SB_DIM)),
            _lspec(l, (SB_DIM, 1)),
            pl.BlockSpec((SB_DIM, SB_DIM), lambda i: (0, 0)),
            _lspec(l, (CONV_W, CONV_DIM)),
            pl.BlockSpec((nsb, CONV_W - 1, CONV_DIM), smap),
            pl.BlockSpec(memory_space=pl.ANY),
            pl.BlockSpec(memory_space=pl.ANY),
        ],
        out_specs=[
            pl.BlockSpec((tm, D_MODEL), row),
            pl.BlockSpec((tm, _RKV), row),
            pl.BlockSpec((tm, SB_DIM), row),
            pl.BlockSpec((tm, CONV_DIM), row),
            pl.BlockSpec((nsb, CONV_W - 1, CONV_DIM), smap),
            tspec,
            tspec,
        ],
        out_shape=[
            jax.ShapeDtypeStruct((n, D_MODEL), F32),
            jax.ShapeDtypeStruct((n, _RKV), F32),
            jax.ShapeDtypeStruct((n, SB_DIM), F32),
            jax.ShapeDtypeStruct((n, CONV_DIM), F32),
            jax.ShapeDtypeStruct((nseq, CONV_W - 1, CONV_DIM), F32),
            jax.ShapeDtypeStruct(kt_all.shape, F32),
            jax.ShapeDtypeStruct(vt_all.shape, F32),
        ],
        input_output_aliases={10: 5, 11: 6},
        scratch_shapes=[pltpu.VMEM((SUBLANES, CONV_DIM), F32)],
        compiler_params=_params(("arbitrary",)),
        name="mix_proj",
    )(x, P["mix_norm"], P["w_rq"], P["w_kvt"], P["w_cv"], P["sb_q_norm"], P["sb_k_norm_col"], C["e384"],
      P["cv_w"], conv0, kt_all, vt_all)


def _state_proj_body(s_ref, w_ref, o_ref):
    o_ref[...] = _dot(s_ref[...].astype(BF16), w_ref[:, 0:_RKV])


def _state_proj(shift0, l, w_rq):
    b = shift0.shape[0]
    return pl.pallas_call(
        _state_proj_body,
        grid=(1,),
        in_specs=[pl.BlockSpec((b, D_MODEL), lambda i: (0, 0)), _lspec(l, (D_MODEL, _RKV + SB_DIM))],
        out_specs=pl.BlockSpec((b, _RKV), lambda i: (0, 0)),
        out_shape=jax.ShapeDtypeStruct((b, _RKV), F32),
        compiler_params=_params(("arbitrary",)),
        name="state_proj",
    )(shift0, w_rq)


def _unit_lower_inverses(lmats):
    n = lmats[0].shape[0]
    eye = (lax.broadcasted_iota(jnp.int32, (n, n), 0) == lax.broadcasted_iota(jnp.int32, (n, n), 1)).astype(F32)
    ps = [eye + lm for lm in lmats]
    lks = []
    for lm in lmats:
        lb = lm.astype(BF16)
        lks.append(_dot(lb, lb))
    power = 2
    while power < CHUNK:
        for t in range(len(lmats)):
            lkb = lks[t].astype(BF16)
            if 2 * power < CHUNK:
                both = _dot(lkb, jnp.concatenate([lks[t], ps[t]], axis=1).astype(BF16))
                lks[t] = both[:, 0:n]
                ps[t] = ps[t] + both[:, n:2 * n]
            else:
                ps[t] = ps[t] + _dot(lkb, ps[t].astype(BF16))
        power *= 2
    return ps


def _state_to_pairs(s_ref, sq):
    zero = jnp.zeros((HEAD_DIM, HEAD_DIM), F32)
    out = []
    for p in range(PAIRS):
        top = jnp.concatenate([s_ref[sq, 2 * p], zero], axis=1)
        bot = jnp.concatenate([zero, s_ref[sq, 2 * p + 1]], axis=1)
        out.append(jnp.concatenate([top, bot], axis=0))
    return out


def _pairs_to_state(sbd, sout_ref, sq):
    for p in range(PAIRS):
        sout_ref[sq, 2 * p] = sbd[p][0:HEAD_DIM, 0:HEAD_DIM]
        sout_ref[sq, 2 * p + 1] = sbd[p][HEAD_DIM:LANES, HEAD_DIM:LANES]


def _rwkv_body(h_ref, rkv_ref, sh0_ref, rp0_ref, s0_ref,
               murkv_ref, muwag_ref, w0_ref, w1_ref, w2_ref, a0_ref, a1_ref, a2_ref, g1_ref, g2_ref,
               kk_ref, ka_ref, rk_ref, lnw_ref, lnb_ref, e_ref, tri_ref,
               o_ref, sout_ref, hl_scr, rl_scr, s_scr, *, tm, rs):
    i = pl.program_id(0)
    seg = min(tm, rs)
    nseg = tm // seg
    nchunk = tm // CHUNK
    carried = rs >= tm
    if carried:
        tiles_per_seq = rs // tm

        @pl.when(i % tiles_per_seq == 0)
        def _():
            hl_scr[SUBLANES - 1:SUBLANES, :] = sh0_ref[0]
            rl_scr[SUBLANES - 1:SUBLANES, :] = rp0_ref[0]
            init = _state_to_pairs(s0_ref, 0)
            for p in range(PAIRS):
                s_scr[p] = init[p]

    h = h_ref[...]
    rkv = rkv_ref[...]
    if carried:
        hprev_rows = hl_scr[SUBLANES - 1:SUBLANES, :]
        rprev_rows = rl_scr[SUBLANES - 1:SUBLANES, :]
    else:
        hprev_rows = jnp.concatenate(
            [jnp.broadcast_to(sh0_ref[s], (seg, D_MODEL)) for s in range(nseg)], axis=0)
        rprev_rows = jnp.concatenate(
            [jnp.broadcast_to(rp0_ref[s], (seg, _RKV)) for s in range(nseg)], axis=0)
    first_h = (lax.broadcasted_iota(jnp.int32, (tm, D_MODEL), 0) & (seg - 1)) == 0
    first_r = (lax.broadcasted_iota(jnp.int32, (tm, _RKV), 0) & (seg - 1)) == 0
    hp = jnp.where(first_h, hprev_rows, pltpu.roll(h, 1, 0))
    rp = jnp.where(first_r, rprev_rows, pltpu.roll(rkv, 1, 0))
    if carried:
        hl_scr[...] = h[tm - SUBLANES:tm]
        rl_scr[...] = rkv[tm - SUBLANES:tm]

    rkv = rkv + murkv_ref[...] * (rp - rkv)
    r = rkv[:, 0:RW_DIM]
    k = rkv[:, RW_DIM:2 * RW_DIM]
    v = rkv[:, 2 * RW_DIM:3 * RW_DIM]
    dx = hp - h
    xw = (h + muwag_ref[0:1, :] * dx).astype(BF16)
    xa = (h + muwag_ref[1:2, :] * dx).astype(BF16)
    xg = (h + muwag_ref[2:3, :] * dx).astype(BF16)
    wl = w0_ref[...] + _dot(jnp.tanh(_dot(xw, w1_ref[...])).astype(BF16), w2_ref[...])
    w = -_softplus(-wl) - 0.5
    logdecay = -jnp.exp(w)
    a = _sigmoid(a0_ref[...] + _dot(_dot(xa, a1_ref[...]).astype(BF16), a2_ref[...]))
    gate = _dot(_sigmoid(_dot(xg, g1_ref[...])).astype(BF16), g2_ref[...])
    e = e_ref[...]
    kk = k * kk_ref[...]
    kk = kk * lax.rsqrt(_dot_exact_rhs(kk * kk, e) + L2_EPS)
    k = k * (1.0 + (a - 1.0) * ka_ref[...])
    b = kk * a
    bonus = _dot_exact_rhs(r * k * rk_ref[...], e) * v

    tri = tri_ref[...]
    ci = lax.broadcasted_iota(jnp.int32, (LANES, LANES), 0)
    cj = lax.broadcasted_iota(jnp.int32, (LANES, LANES), 1)
    strict = cj < ci
    incl = cj <= ci
    lane_lo = _lane_lo()

    work = []
    lmats = []
    for c in range(nchunk):
        sl = slice(c * CHUNK, (c + 1) * CHUNK)
        ld = logdecay[sl]
        cum = _dot_exact_lhs(tri, ld)
        tot = cum[CHUNK - 1:CHUNK, :]
        g_incl = jnp.exp(cum)
        g_prev = jnp.exp(cum - ld)
        g_inv = jnp.exp(-cum)
        g_rem = jnp.exp(tot - cum)
        g_tot = jnp.exp(tot)
        rc, kc, vc, kkc, bc = r[sl], k[sl], v[sl], kk[sl], b[sl]
        rq = rc * g_incl
        kq = kc * g_inv
        bq = bc * g_inv
        aq = -kkc * g_prev
        kz = kc * g_rem
        bz = bc * g_rem
        for p in range(PAIRS):
            cs = slice(p * LANES, (p + 1) * LANES)
            st = lambda x: _pair_stack(x[:, cs], lane_lo).astype(BF16)
            aq_s, rq_s, kq_s, bq_s, kz_s, bz_s, v_s = st(aq), st(rq), st(kq), st(bq), st(kz), st(bz), st(vc)
            quad = _dot_nt(jnp.concatenate([aq_s, rq_s], axis=0), jnp.concatenate([kq_s, bq_s], axis=0))
            m_ak = jnp.where(strict, quad[0:LANES, 0:LANES], 0.0).astype(BF16)
            m_ab = jnp.where(strict, quad[0:LANES, LANES:2 * LANES], 0.0)
            m_rk = jnp.where(incl, quad[LANES:2 * LANES, 0:LANES], 0.0)
            m_rb = jnp.where(incl, quad[LANES:2 * LANES, LANES:2 * LANES], 0.0)
            lmats.append(m_ab)
            work.append(dict(
                aq=aq_s, rq=rq_s, v=v_s, x1=_dot(m_ak, v_s),
                mr=jnp.concatenate([m_rk, m_rb], axis=1).astype(BF16),
                kzbz=jnp.concatenate([kz_s, bz_s], axis=0),
                gtot=g_tot[:, cs]))
    tinvs = _unit_lower_inverses(lmats)
    for wk, tinv in zip(work, tinvs):
        wu = _dot(tinv.astype(BF16), jnp.concatenate([wk["aq"], wk["x1"].astype(BF16)], axis=1))
        wk["wr"] = jnp.concatenate([wu[:, 0:LANES].astype(BF16), wk["rq"]], axis=0)
        wk["u0"] = wu[:, LANES:2 * LANES]

    if carried:
        sbd = [s_scr[p] for p in range(PAIRS)]
    for c in range(nchunk):
        sl = slice(c * CHUNK, (c + 1) * CHUNK)
        sq = (c * CHUNK) // seg
        if not carried:
            sbd = _state_to_pairs(s0_ref, sq)
        outs = []
        for p in range(PAIRS):
            wk = work[c * PAIRS + p]
            y = _dot_nt(wk["wr"], sbd[p].astype(BF16))
            u = y[0:LANES] + wk["u0"]
            vu = jnp.concatenate([wk["v"], u.astype(BF16)], axis=0)
            o_s = y[LANES:2 * LANES] + _dot(wk["mr"], vu)
            sbd[p] = sbd[p] * wk["gtot"] + _dot_tn(vu, wk["kzbz"])
            outs.append(o_s[0:CHUNK] + o_s[CHUNK:2 * CHUNK])
        if not carried:
            _pairs_to_state(sbd, sout_ref, sq)
        o = jnp.concatenate(outs, axis=1)
        mean = _dot_exact_rhs(o, e) * (1.0 / HEAD_DIM)
        d = o - mean
        var = _dot_exact_rhs(d * d, e) * (1.0 / HEAD_DIM)
        on = d * lax.rsqrt(var + GN_EPS) * lnw_ref[...] + lnb_ref[...]
        o_ref[sl, :] = (on + bonus[sl]) * gate[sl]
    if carried:
        for p in range(PAIRS):
            s_scr[p] = sbd[p]
        _pairs_to_state(sbd, sout_ref, 0)


def _rwkv(h, rkv, l, rs, shift0, rkvprev0, s0, P, C):
    n = h.shape[0]
    tm = min(RWKV_TM, n)
    nseq = n // rs
    if rs >= tm:
        tps = rs // tm
        nsb = 1
        smap3 = lambda i: (i // tps, 0, 0)
        smap4 = lambda i: (i // tps, 0, 0, 0)
    else:
        nsb = tm // rs
        smap3 = lambda i: (i, 0, 0)
        smap4 = lambda i: (i, 0, 0, 0)
    row = lambda i: (i, 0)
    lora_w, lora_a, lora_g = P["rw_w1"].shape[-1], P["rw_a1"].shape[-1], P["rw_g1"].shape[-1]
    return pl.pallas_call(
        functools.partial(_rwkv_body, tm=tm, rs=rs),
        grid=(n // tm,),
        in_specs=[
            pl.BlockSpec((tm, D_MODEL), row),
            pl.BlockSpec((tm, _RKV), row),
            pl.BlockSpec((nsb, 1, D_MODEL), smap3),
            pl.BlockSpec((nsb, 1, _RKV), smap3),
            pl.BlockSpec((nsb, RW_HEADS, HEAD_DIM, HEAD_DIM), smap4),
            _lspec(l, (1, _RKV)),
            _lspec(l, (3, D_MODEL)),
            _lspec(l, (1, RW_DIM)),
            _lspec(l, (D_MODEL, lora_w)),
            _lspec(l, (lora_w, RW_DIM)),
            _lspec(l, (1, RW_DIM)),
            _lspec(l, (D_MODEL, lora_a)),
            _lspec(l, (lora_a, RW_DIM)),
            _lspec(l, (D_MODEL, lora_g)),
            _lspec(l, (lora_g, RW_DIM)),
            _lspec(l, (1, RW_DIM)),
            _lspec(l, (1, RW_DIM)),
            _lspec(l, (1, RW_DIM)),
            _lspec(l, (1, RW_DIM)),
            _lspec(l, (1, RW_DIM)),
            pl.BlockSpec((RW_DIM, RW_DIM), lambda i: (0, 0)),
            pl.BlockSpec((CHUNK, CHUNK), lambda i: (0, 0)),
        ],
        out_specs=[
            pl.BlockSpec((tm, RW_DIM), row),
            pl.BlockSpec((nsb, RW_HEADS, HEAD_DIM, HEAD_DIM), smap4),
        ],
        out_shape=[
            jax.ShapeDtypeStruct((n, RW_DIM), F32),
            jax.ShapeDtypeStruct((nseq, RW_HEADS, HEAD_DIM, HEAD_DIM), F32),
        ],
        scratch_shapes=[
            pltpu.VMEM((SUBLANES, D_MODEL), F32),
            pltpu.VMEM((SUBLANES, _RKV), F32),
            pltpu.VMEM((PAIRS, LANES, LANES), F32),
        ],
        compiler_params=_params(("arbitrary",)),
        name="rwkv",
    )(h, rkv, shift0, rkvprev0, s0,
      P["rw_mu_rkv"], P["rw_mu_wag"], P["rw_w0"], P["rw_w1"], P["rw_w2"], P["rw_a0"], P["rw_a1"],
      P["rw_a2"], P["rw_g1"], P["rw_g2"], P["rw_k_k"], P["rw_k_a"], P["rw_r_k"], P["rw_ln_w"],
      P["rw_ln_b"], C["e384"], C["tri"])


def _sb_weights(z, lower, carry, mask):
    sp = jnp.log(1.0 + jnp.exp(-jnp.abs(z)))
    log_beta = jnp.minimum(z, 0.0) - sp
    log_rest = log_beta - z
    if mask is not None:
        log_rest = jnp.where(mask, log_rest, 0.0)
    tail = _dot_exact_rhs(log_rest, lower, terms=2)
    wts = jnp.exp(log_beta + tail + carry)
    if mask is not None:
        wts = jnp.where(mask, wts, 0.0)
    return wts.astype(BF16), carry + tail[:, 0:1] + log_rest[:, 0:1]


def _sb_body(*refs, tq, tkd, n_past, tkp):
    if n_past:
        q_ref, kc_ref, vc_ref, kp_ref, vp_ref, lowd_ref, lowp_ref, o_ref = refs
    else:
        q_ref, kc_ref, vc_ref, lowd_ref, o_ref = refs
    qi = pl.program_id(1)
    lane_lo = _lane_lo()
    rows = SB_HEADS * tq
    qs = [_pair_stack(q_ref[:, p * LANES:(p + 1) * LANES], lane_lo).astype(BF16) for p in range(PAIRS)]

    def block(ref, p, off, tk):
        cols = slice(None) if ref.shape[1] == tk else pl.ds(off, tk)
        return ref[p * LANES:(p + 1) * LANES, cols].astype(BF16)

    def scores(k_ref, off, tk):
        return jnp.concatenate([_dot(qs[p], block(k_ref, p, off, tk)) for p in range(PAIRS)], axis=0)

    def weighted(wts, v_ref, off, tk):
        return jnp.concatenate(
            [_dot_nt(wts[2 * tq * p:2 * tq * (p + 1)], block(v_ref, p, off, tk)) for p in range(PAIRS)], axis=0)

    def step(k_ref, v_ref, off, tk, lower, carry, acc, mask):
        wts, carry = _sb_weights(scores(k_ref, off, tk), lower, carry, mask)
        return carry, acc + weighted(wts, v_ref, off, tk)

    lowd = lowd_ref[...]
    q0 = qi * tq
    jd = q0 // tkd
    offd = pl.multiple_of(jd * tkd, tkd)
    qpos = q0 + (lax.broadcasted_iota(jnp.int32, (rows, tkd), 0) & (tq - 1))
    kpos = offd + lax.broadcasted_iota(jnp.int32, (rows, tkd), 1)
    carry, acc = step(kc_ref, vc_ref, offd, tkd, lowd, jnp.zeros((rows, 1), F32),
                      jnp.zeros((rows, LANES), F32), kpos < qpos)

    def cur_step(jj, st):
        off = pl.multiple_of((jd - 1 - jj) * tkd, tkd)
        return step(kc_ref, vc_ref, off, tkd, lowd, st[0], st[1], None)

    carry, acc = lax.fori_loop(0, jd, cur_step, (carry, acc))
    if n_past:
        lowp = lowp_ref[...]

        def past_step(jj, st):
            off = pl.multiple_of((n_past - 1 - jj) * tkp, tkp)
            return step(kp_ref, vp_ref, off, tkp, lowp, st[0], st[1], None)

        carry, acc = lax.fori_loop(0, n_past, past_step, (carry, acc))
    for p in range(PAIRS):
        base = 2 * tq * p
        o_ref[:, p * LANES:(p + 1) * LANES] = jnp.where(lane_lo, acc[base:base + tq], acc[base + tq:base + 2 * tq])


def _stick_breaking(q, kt, vt, kl, nseq, past_kt, past_vt, pl_, C):
    n = q.shape[0]
    t = n // nseq
    tq = min(SB_TQ, t)
    tkd = min(SB_TK, t)
    nq = t // tq
    qmap = lambda b, i: (b * nq + i, 0)
    cmap = lambda b, i: (kl, b, 0, 0)
    in_specs = [
        pl.BlockSpec((tq, SB_DIM), qmap),
        pl.BlockSpec((None, None, SB_DIM, t), cmap),
        pl.BlockSpec((None, None, SB_DIM, t), cmap),
    ]
    args = [q, kt, vt]
    n_past = 0
    if past_kt is not None:
        plen = past_kt.shape[3]
        n_past = plen // SB_TK
        pmap = lambda b, i: (pl_, b, 0, 0)
        in_specs += [pl.BlockSpec((None, None, SB_DIM, plen), pmap), pl.BlockSpec((None, None, SB_DIM, plen), pmap)]
        args += [past_kt, past_vt]
    in_specs.append(pl.BlockSpec((tkd, tkd), lambda b, i: (0, 0)))
    args.append(C["low%d" % tkd])
    if n_past:
        in_specs.append(pl.BlockSpec((SB_TK, SB_TK), lambda b, i: (0, 0)))
        args.append(C["low%d" % SB_TK])
    return pl.pallas_call(
        functools.partial(_sb_body, tq=tq, tkd=tkd, n_past=n_past, tkp=SB_TK),
        grid=(nseq, nq),
        in_specs=in_specs,
        out_specs=pl.BlockSpec((tq, SB_DIM), qmap),
        out_shape=jax.ShapeDtypeStruct((n, SB_DIM), F32),
        compiler_params=_params(("arbitrary", "arbitrary")),
        name="stick_breaking",
    )(*args)


def _xattn_body(x_ref, orw_ref, osb_ref, ocv_ref, wout_ref, g_ref, wq_ref, qg_ref, mk_ref, mv_ref, wo_ref, o_ref):
    x = x_ref[...]
    x = x + _dot(orw_ref[...].astype(BF16), wout_ref[0:RW_DIM, :])
    x = x + _dot(osb_ref[...].astype(BF16), wout_ref[RW_DIM:RW_DIM + SB_DIM, :])
    x = x + _dot(ocv_ref[...].astype(BF16), wout_ref[RW_DIM + SB_DIM:D_MODEL, :])
    hx = _rms(x, g_ref[...]).astype(BF16)
    qf = _dot(hx, wq_ref[...])
    heads = []
    for hh in range(X_HEADS):
        cs = slice(hh * X_HEAD_DIM, (hh + 1) * X_HEAD_DIM)
        qh = _rms(qf[:, cs], qg_ref[...]).astype(BF16)
        s = _dot_nt(qh, mk_ref[:, cs].astype(BF16)) * (X_HEAD_DIM ** -0.5)
        s = jnp.exp(s - jnp.max(s, axis=-1, keepdims=True))
        attn = s / jnp.sum(s, axis=-1, keepdims=True)
        heads.append(_dot(attn.astype(BF16), mv_ref[:, cs].astype(BF16)))
    o = jnp.concatenate(heads, axis=1).astype(BF16)
    o_ref[...] = x + _dot(o, wo_ref[...])


def _xattn(x, orw, osb, ocv, l, nseq, P, mem_k, mem_v, mem_l):
    n = x.shape[0]
    t = n // nseq
    tq = min(XA_TQ, t)
    nq = t // tq
    nm = mem_k.shape[2]
    row = lambda b, i: (b * nq + i, 0)
    mmap = lambda b, i: (mem_l, b, 0, 0)
    return pl.pallas_call(
        _xattn_body,
        grid=(nseq, nq),
        in_specs=[
            pl.BlockSpec((tq, D_MODEL), row),
            pl.BlockSpec((tq, RW_DIM), row),
            pl.BlockSpec((tq, SB_DIM), row),
            pl.BlockSpec((tq, CONV_DIM), row),
            _lspec(l, (D_MODEL, D_MODEL)),
            _lspec(l, (1, D_MODEL)),
            _lspec(l, (D_MODEL, D_MODEL)),
            _lspec(l, (1, X_HEAD_DIM)),
            pl.BlockSpec((None, None, nm, D_MODEL), mmap),
            pl.BlockSpec((None, None, nm, D_MODEL), mmap),
            _lspec(l, (D_MODEL, D_MODEL)),
        ],
        out_specs=pl.BlockSpec((tq, D_MODEL), row),
        out_shape=jax.ShapeDtypeStruct((n, D_MODEL), F32),
        compiler_params=_params(("arbitrary", "arbitrary")),
        name="xattn",
    )(x, orw, osb, ocv, P["w_out"], P["x_norm"], P["x_wq"], P["x_q_norm"], mem_k, mem_v, P["x_wo"])


def _memkv_body(m_ref, g_ref, wk_ref, wv_ref, kg_ref, mk_ref, mv_ref):
    m = _rms(m_ref[...], g_ref[...]).astype(BF16)
    kf = _dot(m, wk_ref[...])
    for hh in range(X_HEADS):
        cs = slice(hh * X_HEAD_DIM, (hh + 1) * X_HEAD_DIM)
        mk_ref[:, cs] = _rms(kf[:, cs], kg_ref[...])
    mv_ref[...] = _dot(m, wv_ref[...])


def _memory_kv(mem, l, P):
    n = mem.shape[0]
    tm = min(MEM_TM, n)
    row = lambda i: (i, 0)
    return pl.pallas_call(
        _memkv_body,
        grid=(n // tm,),
        in_specs=[
            pl.BlockSpec((tm, D_MODEL), row),
            _lspec(l, (1, D_MODEL)),
            _lspec(l, (D_MODEL, D_MODEL)),
            _lspec(l, (D_MODEL, D_MODEL)),
            _lspec(l, (1, X_HEAD_DIM)),
        ],
        out_specs=[pl.BlockSpec((tm, D_MODEL), row), pl.BlockSpec((tm, D_MODEL), row)],
        out_shape=[jax.ShapeDtypeStruct((n, D_MODEL), F32), jax.ShapeDtypeStruct((n, D_MODEL), F32)],
        compiler_params=_params(("arbitrary",)),
        name="memory_kv",
    )(mem, P["mem_norm"], P["x_wk"], P["x_wv"], P["x_k_norm"])


def _layer(x, l, nseq, mem_k, mem_v, mem_l, past_kt, past_vt, s0, shift0, rkvprev0, conv0, kt_all, vt_all, P, C):
    n = x.shape[0]
    rs = n // nseq
    x = _ffn(x, l, P["ffn1_norm"], P["ffn1_wg"], P["ffn1_wu"], P["ffn1_wd"])
    h, rkv, q, o_cv, conv_buf, kt_all, vt_all = _mix_proj(x, l, rs, P, C, conv0, kt_all, vt_all)
    o_rw, s_t = _rwkv(h, rkv, l, rs, shift0, rkvprev0, s0, P, C)
    if kt_all.shape[1] == nseq:
        kt_cur, vt_cur, kl = kt_all, vt_all, l
    else:
        def per_seq(a):
            return a[l, 0].reshape(SB_DIM, nseq, rs).transpose(1, 0, 2)[None]
        kt_cur, vt_cur, kl = per_seq(kt_all), per_seq(vt_all), 0
    o_sb = _stick_breaking(q, kt_cur, vt_cur, kl, nseq, past_kt, past_vt, l, C)
    x = _xattn(x, o_rw, o_sb, o_cv, l, nseq, P, mem_k, mem_v, mem_l)
    x = _ffn(x, l, P["ffn2_norm"], P["ffn2_wg"], P["ffn2_wu"], P["ffn2_wd"])
    h_last = h.reshape(nseq, rs, D_MODEL)[:, rs - 1]
    return x, s_t, h_last, conv_buf, kt_all, vt_all


def _lower_ones(n, strict):
    i = lax.broadcasted_iota(jnp.int32, (n, n), 0)
    j = lax.broadcasted_iota(jnp.int32, (n, n), 1)
    return ((j < i) if strict else (j <= i)).astype(BF16)


def kernel(x_prompt, x_sample, mem_prompt, cache_sb_k, cache_sb_v, state_rwkv, state_shift, state_conv,
           cache_mem_k, cache_mem_v, ffn1_norm, ffn1_wg, ffn1_wu, ffn1_wd, mix_norm, w_in, w_out, rw_mu_rkv,
           rw_mu_wag, rw_w0, rw_w1, rw_w2, rw_a0, rw_a1, rw_a2, rw_g1, rw_g2, rw_k_k, rw_k_a, rw_r_k, rw_ln_w,
           rw_ln_b, sb_q_norm, sb_k_norm, cv_w, x_norm, mem_norm, x_wq, x_wk, x_wv, x_wo, x_q_norm, x_k_norm,
           ffn2_norm, ffn2_wg, ffn2_wu, ffn2_wd):
    depth = w_in.shape[0]
    bp, tp, _ = x_prompt.shape
    bs, ts, _ = x_sample.shape
    n_mem = mem_prompt.shape[1]
    plen = cache_sb_k.shape[2]

    def vec(a):
        return a.reshape(a.shape[0], 1, -1)

    def per_head(a):
        return jnp.tile(a, (1, SB_HEADS))

    w_in_b = w_in.astype(BF16)
    P = dict(
        ffn1_norm=vec(ffn1_norm), ffn1_wg=ffn1_wg.astype(BF16), ffn1_wu=ffn1_wu.astype(BF16),
        ffn1_wd=ffn1_wd.astype(BF16),
        mix_norm=vec(mix_norm), w_rq=w_in_b[:, :, 0:_K0],
        w_kvt=jnp.swapaxes(w_in_b[:, :, _K0:_C0], 1, 2), w_cv=w_in_b[:, :, _C0:],
        w_out=w_out.astype(BF16),
        rw_mu_rkv=rw_mu_rkv.reshape(depth, 1, _RKV), rw_mu_wag=rw_mu_wag,
        rw_w0=vec(rw_w0), rw_w1=rw_w1.astype(BF16), rw_w2=rw_w2.astype(BF16),
        rw_a0=vec(rw_a0), rw_a1=rw_a1.astype(BF16), rw_a2=rw_a2.astype(BF16),
        rw_g1=rw_g1.astype(BF16), rw_g2=rw_g2.astype(BF16),
        rw_k_k=vec(rw_k_k), rw_k_a=vec(rw_k_a), rw_r_k=rw_r_k.reshape(depth, 1, RW_DIM),
        rw_ln_w=vec(rw_ln_w), rw_ln_b=vec(rw_ln_b),
        sb_q_norm=vec(per_head(sb_q_norm)), sb_k_norm_col=per_head(sb_k_norm).reshape(depth, SB_DIM, 1), cv_w=cv_w,
        x_norm=vec(x_norm), mem_norm=vec(mem_norm),
        x_wq=x_wq.astype(BF16), x_wk=x_wk.astype(BF16), x_wv=x_wv.astype(BF16), x_wo=x_wo.astype(BF16),
        x_q_norm=vec(x_q_norm), x_k_norm=vec(x_k_norm),
        ffn2_norm=vec(ffn2_norm), ffn2_wg=ffn2_wg.astype(BF16), ffn2_wu=ffn2_wu.astype(BF16),
        ffn2_wd=ffn2_wd.astype(BF16),
    )
    hid = lax.broadcasted_iota(jnp.int32, (RW_DIM, RW_DIM), 0) // HEAD_DIM
    hjd = lax.broadcasted_iota(jnp.int32, (RW_DIM, RW_DIM), 1) // HEAD_DIM
    C = {"e384": (hid == hjd).astype(BF16), "tri": _lower_ones(CHUNK, False)}
    for size in {min(SB_TK, tp), min(SB_TK, ts), SB_TK}:
        C["low%d" % size] = _lower_ones(size, True)

    dt = x_prompt.dtype
    zero_s = jnp.zeros((bp, RW_HEADS, HEAD_DIM, HEAD_DIM), state_rwkv.dtype)
    zero_shift = jnp.zeros((bp, 1, D_MODEL), dt)
    zero_rkv = jnp.zeros((bp, 1, _RKV), dt)
    zero_conv = jnp.zeros((bp, CONV_W - 1, CONV_DIM), dt)
    past_kt = jnp.transpose(cache_sb_k, (0, 1, 3, 4, 2)).reshape(depth, bs, SB_DIM, plen)
    past_vt = jnp.transpose(cache_sb_v, (0, 1, 3, 4, 2)).reshape(depth, bs, SB_DIM, plen)
    cmk = cache_mem_k.reshape(depth, bs, n_mem, D_MODEL)
    cmv = cache_mem_v.reshape(depth, bs, n_mem, D_MODEL)

    yp = x_prompt.reshape(bp * tp, D_MODEL)
    ys = x_sample.reshape(bs * ts, D_MODEL)
    mem = mem_prompt.reshape(bp * n_mem, D_MODEL)
    pkt = jnp.zeros((depth, bp, SB_DIM, tp), dt)
    pvt = jnp.zeros((depth, bp, SB_DIM, tp), dt)
    skt = jnp.zeros((depth, 1, SB_DIM, bs * ts), dt)
    svt = jnp.zeros((depth, 1, SB_DIM, bs * ts), dt)
    pS, psh, pcv, pmk, pmv = [], [], [], [], []
    sS, ssh, scv = [], [], []
    for l in range(depth):
        mk, mv = _memory_kv(mem, l, P)
        mk4 = mk.reshape(1, bp, n_mem, D_MODEL)
        mv4 = mv.reshape(1, bp, n_mem, D_MODEL)
        yp, s_t, sh, cbuf, pkt, pvt = _layer(yp, l, bp, mk4, mv4, 0, None, None, zero_s, zero_shift, zero_rkv,
                                             zero_conv, pkt, pvt, P, C)
        pS.append(s_t); psh.append(sh); pcv.append(cbuf); pmk.append(mk); pmv.append(mv)
        shift_l = state_shift[l]
        rkvprev = _state_proj(shift_l, l, P["w_rq"]).reshape(bs, 1, _RKV)
        ys, s_t, sh, cbuf, skt, svt = _layer(ys, l, bs, cmk, cmv, l, past_kt, past_vt, state_rwkv[l],
                                             shift_l.reshape(bs, 1, D_MODEL), rkvprev, state_conv[l], skt, svt, P, C)
        sS.append(s_t); ssh.append(sh); scv.append(cbuf)

    def stack(xs, shape):
        return jnp.stack(xs).reshape((depth,) + shape)

    def prompt_kv(a):
        return jnp.transpose(a.reshape(depth, bp, SB_HEADS, HEAD_DIM, tp), (0, 1, 4, 2, 3))

    def sample_kv(a):
        return jnp.transpose(a.reshape(depth, SB_HEADS, HEAD_DIM, bs, ts), (0, 3, 4, 1, 2))

    return (
        yp.reshape(bp, tp, D_MODEL), ys.reshape(bs, ts, D_MODEL),
        prompt_kv(pkt), prompt_kv(pvt),
        stack(pS, (bp, RW_HEADS, HEAD_DIM, HEAD_DIM)), stack(psh, (bp, D_MODEL)),
        stack(pcv, (bp, CONV_W - 1, CONV_DIM)),
        stack(pmk, (bp, n_mem, X_HEADS, X_HEAD_DIM)), stack(pmv, (bp, n_mem, X_HEADS, X_HEAD_DIM)),
        sample_kv(skt), sample_kv(svt),
        stack(sS, (bs, RW_HEADS, HEAD_DIM, HEAD_DIM)), stack(ssh, (bs, D_MODEL)),
        stack(scv, (bs, CONV_W - 1, CONV_DIM)),
    )
```

```python
import functools

import jax
import jax.numpy as jnp
from jax import lax
from jax.experimental import pallas as pl
from jax.experimental.pallas import tpu as pltpu

F32 = jnp.float32
BF16 = jnp.bfloat16

D_MODEL = 1024
HEAD_DIM = 64
RW_HEADS = 6
SB_HEADS = 6
RW_DIM = RW_HEADS * HEAD_DIM
SB_DIM = SB_HEADS * HEAD_DIM
CONV_DIM = 256
CONV_W = 3
FFN_DIM = 2816
X_HEADS = 4
X_HEAD_DIM = D_MODEL // X_HEADS
RMS_EPS = 1e-6
GN_EPS = 64e-5
L2_EPS = 1e-12
LOG2E = 1.4426950408889634

CHUNK = 64
LANES = 128
SUBLANES = 8
PAIRS = RW_DIM // LANES
VMEM_LIMIT = 56 * 1024 * 1024

FFN_TM = 1024
FFN_TF = 256
MIX_TM = 512
RWKV_TM = 256
SB_TQ = 128
SB_TK = 256
XA_TQ = 512
MEM_TM = 512

_RKV = 3 * RW_DIM
_Q0 = _RKV
_K0 = _Q0 + SB_DIM
_V0 = _K0 + SB_DIM
_C0 = _V0 + SB_DIM


def _dot(a, b):
    return jnp.dot(a, b, preferred_element_type=F32)


def _dot_nt(a, b):
    return lax.dot_general(a, b, (((1,), (1,)), ((), ())), preferred_element_type=F32)


def _dot_tn(a, b):
    return lax.dot_general(a, b, (((0,), (0,)), ((), ())), preferred_element_type=F32)


def _split(x, terms):
    parts = []
    rem = x
    for t in range(terms):
        p = rem.astype(BF16)
        parts.append(p)
        if t + 1 < terms:
            rem = rem - p.astype(F32)
    return parts


def _dot_exact_rhs(x, m, terms=3):
    acc = None
    for p in _split(x, terms):
        y = _dot(p, m)
        acc = y if acc is None else acc + y
    return acc


def _dot_exact_lhs(m, x, terms=3):
    acc = None
    for p in _split(x, terms):
        y = _dot(m, p)
        acc = y if acc is None else acc + y
    return acc


def _head_sums(x, e_pair, terms=2):
    outs = []
    for p in range(x.shape[1] // LANES):
        outs.append(_dot_exact_rhs(x[:, p * LANES:(p + 1) * LANES], e_pair, terms))
    return jnp.concatenate(outs, axis=1)


def _rms(x, g):
    return x * lax.rsqrt(jnp.mean(x * x, axis=-1, keepdims=True) + RMS_EPS) * g


def _sigmoid(x):
    return 1.0 / (1.0 + jnp.exp(-x))


def _softplus(y):
    return jnp.maximum(y, 0.0) + jnp.log1p(jnp.exp(-jnp.abs(y)))


def _params(sem):
    return pltpu.CompilerParams(dimension_semantics=sem, vmem_limit_bytes=VMEM_LIMIT)


def _lspec(l, tail):
    nz = (0,) * len(tail)
    return pl.BlockSpec((None,) + tuple(tail), lambda *_: (l,) + nz)


def _lane_lo():
    return lax.broadcasted_iota(jnp.int32, (1, LANES), 1) < HEAD_DIM


def _pair_stack(xp, lane_lo):
    return jnp.concatenate([jnp.where(lane_lo, xp, 0.0), jnp.where(lane_lo, 0.0, xp)], axis=0)


def _ffn_body(x_ref, g_ref, wg_ref, wu_ref, wd_ref, o_ref):
    x = x_ref[...]
    h = _rms(x, g_ref[...]).astype(BF16)
    acc = None
    for f in range(FFN_DIM // FFN_TF):
        fs = slice(f * FFN_TF, (f + 1) * FFN_TF)
        gate = _dot(h, wg_ref[:, fs])
        up = _dot(h, wu_ref[:, fs])
        act = (gate * _sigmoid(gate) * up).astype(BF16)
        part = _dot(act, wd_ref[fs, :])
        acc = part if acc is None else acc + part
    o_ref[...] = x + 0.5 * acc


def _ffn(x, l, norm, wg, wu, wd):
    n = x.shape[0]
    tm = min(FFN_TM, n)
    once = pl.Buffered(1)
    return pl.pallas_call(
        _ffn_body,
        grid=(n // tm,),
        in_specs=[
            pl.BlockSpec((tm, D_MODEL), lambda i: (i, 0)),
            _lspec(l, (1, D_MODEL)),
            pl.BlockSpec((None, D_MODEL, FFN_DIM), lambda i: (l, 0, 0), pipeline_mode=once),
            pl.BlockSpec((None, D_MODEL, FFN_DIM), lambda i: (l, 0, 0), pipeline_mode=once),
            pl.BlockSpec((None, FFN_DIM, D_MODEL), lambda i: (l, 0, 0), pipeline_mode=once),
        ],
        out_specs=pl.BlockSpec((tm, D_MODEL), lambda i: (i, 0)),
        out_shape=jax.ShapeDtypeStruct((n, D_MODEL), F32),
        compiler_params=_params(("arbitrary",)),
        name="ffn",
    )(x, norm, wg, wu, wd)


def _mix_body(x_ref, g_ref, wrq_ref, wkv_ref, wcv_ref, qg_ref, kg_ref, e_ref, cw_ref, c0_ref, kin_ref, vin_ref,
              h_ref, rkv_ref, q_ref, ocv_ref, cbuf_ref, kt_ref, vt_ref, tail_scr, *, tm, rs):
    del kin_ref, vin_ref
    i = pl.program_id(0)
    h = _rms(x_ref[...], g_ref[...])
    h_ref[...] = h
    hb = h.astype(BF16)
    rkv_ref[...] = _dot(hb, wrq_ref[:, 0:_RKV])
    sq = _dot(hb, wrq_ref[:, _RKV:_RKV + SB_DIM])
    ms = _head_sums(sq * sq, e_ref[...], terms=3) * (1.0 / HEAD_DIM)
    q_ref[...] = sq * lax.rsqrt(ms + RMS_EPS) * qg_ref[...] * (HEAD_DIM ** -0.5 * LOG2E)

    kt = _dot_nt(wkv_ref[0:SB_DIM, :], hb)
    k3 = kt.reshape(SB_HEADS, HEAD_DIM, tm)
    kms = jnp.mean(k3 * k3, axis=1, keepdims=True)
    kt_ref[...] = (k3 * lax.rsqrt(kms + RMS_EPS)).reshape(SB_DIM, tm) * kg_ref[...]
    vt_ref[...] = _dot_nt(wkv_ref[SB_DIM:2 * SB_DIM, :], hb)

    cb = _dot(hb, wcv_ref[:, 0:CONV_DIM])
    u = _dot(hb, wcv_ref[:, CONV_DIM:2 * CONV_DIM]) * _dot(hb, wcv_ref[:, 2 * CONV_DIM:3 * CONV_DIM])
    w0 = cw_ref[0:1, :]
    w1 = cw_ref[1:2, :]
    w2 = cw_ref[2:3, :]
    seg = min(tm, rs)
    nseg = tm // seg
    if rs >= tm:
        tiles_per_seq = rs // tm

        @pl.when(i % tiles_per_seq == 0)
        def _():
            tail_scr[SUBLANES - 2:SUBLANES, :] = c0_ref[0]

    row = lax.broadcasted_iota(jnp.int32, (seg, CONV_DIM), 0)
    for s in range(nseg):
        us = u[s * seg:(s + 1) * seg]
        if rs >= tm:
            b0 = tail_scr[SUBLANES - 2:SUBLANES - 1, :]
            b1 = tail_scr[SUBLANES - 1:SUBLANES, :]
        else:
            b0 = c0_ref[s, 0:1, :]
            b1 = c0_ref[s, 1:2, :]
        p1 = jnp.where(row == 0, b1, pltpu.roll(us, 1, 0))
        p2 = jnp.where(row == 0, b0, jnp.where(row == 1, b1, pltpu.roll(us, 2, 0)))
        y = p2 * w0 + p1 * w1 + us * w2
        ocv_ref[s * seg:(s + 1) * seg, :] = cb[s * seg:(s + 1) * seg] * y
        tail_scr[...] = us[seg - SUBLANES:seg]
        cbuf_ref[s] = tail_scr[SUBLANES - 2:SUBLANES, :]


def _mix_proj(x, l, rs, P, C, conv0, kt_all, vt_all):
    n = x.shape[0]
    tm = min(MIX_TM, n)
    nseq = n // rs
    if rs >= tm:
        tps = rs // tm
        nsb = 1
        smap = lambda i: (i // tps, 0, 0)
        tmap = lambda i: (l, i // tps, 0, i % tps)
    else:
        nsb = tm // rs
        smap = lambda i: (i, 0, 0)
        tmap = lambda i: (l, 0, 0, i)
    row = lambda i: (i, 0)
    tspec = pl.BlockSpec((None, None, SB_DIM, tm), tmap)
    return pl.pallas_call(
        functools.partial(_mix_body, tm=tm, rs=rs),
        grid=(n // tm,),
        in_specs=[
            pl.BlockSpec((tm, D_MODEL), row),
            _lspec(l, (1, D_MODEL)),
            _lspec(l, (D_MODEL, _RKV + SB_DIM)),
            _lspec(l, (2 * SB_DIM, D_MODEL)),
            _lspec(l, (D_MODEL, 3 * CONV_DIM)),
            _lspec(l, (1, SB_DIM)),
            _lspec(l, (SB_DIM, 1)),
            pl.BlockSpec((LANES, LANES), lambda i: (0, 0)),
            _lspec(l, (CONV_W, CONV_DIM)),
            pl.BlockSpec((nsb, CONV_W - 1, CONV_DIM), smap),
            pl.BlockSpec(memory_space=pl.ANY),
            pl.BlockSpec(memory_space=pl.ANY),
        ],
        out_specs=[
            pl.BlockSpec((tm, D_MODEL), row),
            pl.BlockSpec((tm, _RKV), row),
            pl.BlockSpec((tm, SB_DIM), row),
            pl.BlockSpec((tm, CONV_DIM), row),
            pl.BlockSpec((nsb, CONV_W - 1, CONV_DIM), smap),
            tspec,
            tspec,
        ],
        out_shape=[
            jax.ShapeDtypeStruct((n, D_MODEL), F32),
            jax.ShapeDtypeStruct((n, _RKV), F32),
            jax.ShapeDtypeStruct((n, SB_DIM), F32),
            jax.ShapeDtypeStruct((n, CONV_DIM), F32),
            jax.ShapeDtypeStruct((nseq, CONV_W - 1, CONV_DIM), F32),
            jax.ShapeDtypeStruct(kt_all.shape, F32),
            jax.ShapeDtypeStruct(vt_all.shape, F32),
        ],
        input_output_aliases={10: 5, 11: 6},
        scratch_shapes=[pltpu.VMEM((SUBLANES, CONV_DIM), F32)],
        compiler_params=_params(("arbitrary",)),
        name="mix_proj",
    )(x, P["mix_norm"], P["w_rq"], P["w_kvt"], P["w_cv"], P["sb_q_norm"], P["sb_k_norm_col"], C["e128"],
      P["cv_w"], conv0, kt_all, vt_all)


def _state_proj_body(s_ref, w_ref, o_ref):
    o_ref[...] = _dot(s_ref[...].astype(BF16), w_ref[:, 0:_RKV])


def _state_proj(shift0, l, w_rq):
    b = shift0.shape[0]
    return pl.pallas_call(
        _state_proj_body,
        grid=(1,),
        in_specs=[pl.BlockSpec((b, D_MODEL), lambda i: (0, 0)), _lspec(l, (D_MODEL, _RKV + SB_DIM))],
        out_specs=pl.BlockSpec((b, _RKV), lambda i: (0, 0)),
        out_shape=jax.ShapeDtypeStruct((b, _RKV), F32),
        compiler_params=_params(("arbitrary",)),
        name="state_proj",
    )(shift0, w_rq)


def _unit_lower_inverses(lmats):
    n = lmats[0].shape[0]
    eye = (lax.broadcasted_iota(jnp.int32, (n, n), 0) == lax.broadcasted_iota(jnp.int32, (n, n), 1)).astype(F32)
    ps = [eye + lm for lm in lmats]
    lks = []
    for lm in lmats:
        lb = lm.astype(BF16)
        lks.append(_dot(lb, lb))
    power = 2
    while power < CHUNK:
        for t in range(len(lmats)):
            lkb = lks[t].astype(BF16)
            if 2 * power < CHUNK:
                both = _dot(lkb, jnp.concatenate([lks[t], ps[t]], axis=1).astype(BF16))
                lks[t] = both[:, 0:n]
                ps[t] = ps[t] + both[:, n:2 * n]
            else:
                ps[t] = ps[t] + _dot(lkb, ps[t].astype(BF16))
        power *= 2
    return ps


def _state_to_pairs(s_ref, sq):
    zero = jnp.zeros((HEAD_DIM, HEAD_DIM), F32)
    out = []
    for p in range(PAIRS):
        top = jnp.concatenate([s_ref[sq, 2 * p], zero], axis=1)
        bot = jnp.concatenate([zero, s_ref[sq, 2 * p + 1]], axis=1)
        out.append(jnp.concatenate([top, bot], axis=0))
    return out


def _pairs_to_state(sbd, sout_ref, sq):
    for p in range(PAIRS):
        sout_ref[sq, 2 * p] = sbd[p][0:HEAD_DIM, 0:HEAD_DIM]
        sout_ref[sq, 2 * p + 1] = sbd[p][HEAD_DIM:LANES, HEAD_DIM:LANES]


def _rwkv_body(h_ref, rkv_ref, sh0_ref, rp0_ref, s0_ref,
               murkv_ref, muwag_ref, w0_ref, w1_ref, w2_ref, a0_ref, a1_ref, a2_ref, g1_ref, g2_ref,
               kk_ref, ka_ref, rk_ref, lnw_ref, lnb_ref, e_ref, csum_ref,
               o_ref, sout_ref, hl_scr, rl_scr, s_scr, *, tm, rs):
    i = pl.program_id(0)
    seg = min(tm, rs)
    nseg = tm // seg
    nchunk = tm // CHUNK
    carried = rs >= tm
    if carried:
        tiles_per_seq = rs // tm

        @pl.when(i % tiles_per_seq == 0)
        def _():
            hl_scr[SUBLANES - 1:SUBLANES, :] = sh0_ref[0]
            rl_scr[SUBLANES - 1:SUBLANES, :] = rp0_ref[0]
            init = _state_to_pairs(s0_ref, 0)
            for p in range(PAIRS):
                s_scr[p] = init[p]

    h = h_ref[...]
    rkv = rkv_ref[...]
    if carried:
        hprev_rows = hl_scr[SUBLANES - 1:SUBLANES, :]
        rprev_rows = rl_scr[SUBLANES - 1:SUBLANES, :]
    else:
        hprev_rows = jnp.concatenate(
            [jnp.broadcast_to(sh0_ref[s], (seg, D_MODEL)) for s in range(nseg)], axis=0)
        rprev_rows = jnp.concatenate(
            [jnp.broadcast_to(rp0_ref[s], (seg, _RKV)) for s in range(nseg)], axis=0)
    first_h = (lax.broadcasted_iota(jnp.int32, (tm, D_MODEL), 0) & (seg - 1)) == 0
    first_r = (lax.broadcasted_iota(jnp.int32, (tm, _RKV), 0) & (seg - 1)) == 0
    hp = jnp.where(first_h, hprev_rows, pltpu.roll(h, 1, 0))
    rp = jnp.where(first_r, rprev_rows, pltpu.roll(rkv, 1, 0))
    if carried:
        hl_scr[...] = h[tm - SUBLANES:tm]
        rl_scr[...] = rkv[tm - SUBLANES:tm]

    rkv = rkv + murkv_ref[...] * (rp - rkv)
    r = rkv[:, 0:RW_DIM]
    k = rkv[:, RW_DIM:2 * RW_DIM]
    v = rkv[:, 2 * RW_DIM:3 * RW_DIM]
    dx = hp - h
    xw = (h + muwag_ref[0:1, :] * dx).astype(BF16)
    xa = (h + muwag_ref[1:2, :] * dx).astype(BF16)
    xg = (h + muwag_ref[2:3, :] * dx).astype(BF16)
    wl = w0_ref[...] + _dot(jnp.tanh(_dot(xw, w1_ref[...])).astype(BF16), w2_ref[...])
    w = -_softplus(-wl) - 0.5
    logdecay = -jnp.exp(w)
    a = _sigmoid(a0_ref[...] + _dot(_dot(xa, a1_ref[...]).astype(BF16), a2_ref[...]))
    gate = _dot(_sigmoid(_dot(xg, g1_ref[...])).astype(BF16), g2_ref[...])
    e = e_ref[...]
    kk = k * kk_ref[...]
    kk = kk * lax.rsqrt(_head_sums(kk * kk, e) + L2_EPS)
    k = k * (1.0 + (a - 1.0) * ka_ref[...])
    b = kk * a
    bonus = _head_sums(r * k * rk_ref[...], e) * v

    ci = lax.broadcasted_iota(jnp.int32, (LANES, LANES), 0)
    cj = lax.broadcasted_iota(jnp.int32, (LANES, LANES), 1)
    strict = cj < ci
    incl = cj <= ci
    lane_lo = _lane_lo()

    sums = _dot_exact_lhs(csum_ref[...], logdecay)
    cum = sums[0:tm]
    tot = sums[tm:2 * tm]
    g_incl = jnp.exp(cum)
    g_prev = jnp.exp(cum - logdecay)
    g_inv = jnp.exp(-cum)
    g_rem = jnp.exp(tot - cum)
    g_tot = jnp.exp(tot)
    rq = r * g_incl
    kq = k * g_inv
    bq = b * g_inv
    aq = -kk * g_prev
    kz = k * g_rem
    bz = b * g_rem
    items = [(c, p) for c in range(nchunk) for p in range(PAIRS)]

    def stacked(x, c, p):
        return _pair_stack(x[c * CHUNK:(c + 1) * CHUNK, p * LANES:(p + 1) * LANES], lane_lo).astype(BF16)

    work = [dict(aq=stacked(aq, c, p), rq=stacked(rq, c, p), v=stacked(v, c, p),
                 kzbz=jnp.concatenate([stacked(kz, c, p), stacked(bz, c, p)], axis=0),
                 gtot=g_tot[c * CHUNK:c * CHUNK + 1, p * LANES:(p + 1) * LANES]) for c, p in items]
    quads = [_dot_nt(jnp.concatenate([wk["aq"], wk["rq"]], axis=0),
                     jnp.concatenate([stacked(kq, c, p), stacked(bq, c, p)], axis=0))
             for wk, (c, p) in zip(work, items)]
    lmats = []
    for wk, quad in zip(work, quads):
        wk["m_ak"] = jnp.where(strict, quad[0:LANES, 0:LANES], 0.0).astype(BF16)
        lmats.append(jnp.where(strict, quad[0:LANES, LANES:2 * LANES], 0.0))
        wk["mr"] = jnp.concatenate([jnp.where(incl, quad[LANES:2 * LANES, 0:LANES], 0.0),
                                    jnp.where(incl, quad[LANES:2 * LANES, LANES:2 * LANES], 0.0)],
                                   axis=1).astype(BF16)
    for wk in work:
        wk["x1"] = _dot(wk["m_ak"], wk["v"])
    tinvs = _unit_lower_inverses(lmats)
    for wk, tinv in zip(work, tinvs):
        wu = _dot(tinv.astype(BF16), jnp.concatenate([wk["aq"], wk["x1"].astype(BF16)], axis=1))
        wk["wr"] = jnp.concatenate([wu[:, 0:LANES].astype(BF16), wk["rq"]], axis=0)
        wk["u0"] = wu[:, LANES:2 * LANES]

    if carried:
        sbd = [s_scr[p] for p in range(PAIRS)]
    o_chunks = []
    for c in range(nchunk):
        sq = (c * CHUNK) // seg
        if not carried:
            sbd = _state_to_pairs(s0_ref, sq)
        wks = work[c * PAIRS:(c + 1) * PAIRS]
        ys = [_dot_nt(wks[p]["wr"], sbd[p].astype(BF16)) for p in range(PAIRS)]
        vus = [jnp.concatenate([wks[p]["v"], (ys[p][0:LANES] + wks[p]["u0"]).astype(BF16)], axis=0)
               for p in range(PAIRS)]
        sbd = [sbd[p] * wks[p]["gtot"] + _dot_tn(vus[p], wks[p]["kzbz"]) for p in range(PAIRS)]
        o_ss = [ys[p][LANES:2 * LANES] + _dot(wks[p]["mr"], vus[p]) for p in range(PAIRS)]
        if not carried:
            _pairs_to_state(sbd, sout_ref, sq)
        o_chunks.append(jnp.concatenate([o_s[0:CHUNK] + o_s[CHUNK:2 * CHUNK] for o_s in o_ss], axis=1))
    if carried:
        for p in range(PAIRS):
            s_scr[p] = sbd[p]
        _pairs_to_state(sbd, sout_ref, 0)
    o = jnp.concatenate(o_chunks, axis=0)
    mean = _head_sums(o, e) * (1.0 / HEAD_DIM)
    d = o - mean
    var = _head_sums(d * d, e) * (1.0 / HEAD_DIM)
    on = d * lax.rsqrt(var + GN_EPS) * lnw_ref[...] + lnb_ref[...]
    o_ref[...] = (on + bonus) * gate


def _rwkv(h, rkv, l, rs, shift0, rkvprev0, s0, P, C):
    n = h.shape[0]
    tm = min(RWKV_TM, n)
    nseq = n // rs
    if rs >= tm:
        tps = rs // tm
        nsb = 1
        smap3 = lambda i: (i // tps, 0, 0)
        smap4 = lambda i: (i // tps, 0, 0, 0)
    else:
        nsb = tm // rs
        smap3 = lambda i: (i, 0, 0)
        smap4 = lambda i: (i, 0, 0, 0)
    row = lambda i: (i, 0)
    lora_w, lora_a, lora_g = P["rw_w1"].shape[-1], P["rw_a1"].shape[-1], P["rw_g1"].shape[-1]
    return pl.pallas_call(
        functools.partial(_rwkv_body, tm=tm, rs=rs),
        grid=(n // tm,),
        in_specs=[
            pl.BlockSpec((tm, D_MODEL), row),
            pl.BlockSpec((tm, _RKV), row),
            pl.BlockSpec((nsb, 1, D_MODEL), smap3),
            pl.BlockSpec((nsb, 1, _RKV), smap3),
            pl.BlockSpec((nsb, RW_HEADS, HEAD_DIM, HEAD_DIM), smap4),
            _lspec(l, (1, _RKV)),
            _lspec(l, (3, D_MODEL)),
            _lspec(l, (1, RW_DIM)),
            _lspec(l, (D_MODEL, lora_w)),
            _lspec(l, (lora_w, RW_DIM)),
            _lspec(l, (1, RW_DIM)),
            _lspec(l, (D_MODEL, lora_a)),
            _lspec(l, (lora_a, RW_DIM)),
            _lspec(l, (D_MODEL, lora_g)),
            _lspec(l, (lora_g, RW_DIM)),
            _lspec(l, (1, RW_DIM)),
            _lspec(l, (1, RW_DIM)),
            _lspec(l, (1, RW_DIM)),
            _lspec(l, (1, RW_DIM)),
            _lspec(l, (1, RW_DIM)),
            pl.BlockSpec((LANES, LANES), lambda i: (0, 0)),
            pl.BlockSpec((2 * tm, tm), lambda i: (0, 0)),
        ],
        out_specs=[
            pl.BlockSpec((tm, RW_DIM), row),
            pl.BlockSpec((nsb, RW_HEADS, HEAD_DIM, HEAD_DIM), smap4),
        ],
        out_shape=[
            jax.ShapeDtypeStruct((n, RW_DIM), F32),
            jax.ShapeDtypeStruct((nseq, RW_HEADS, HEAD_DIM, HEAD_DIM), F32),
        ],
        scratch_shapes=[
            pltpu.VMEM((SUBLANES, D_MODEL), F32),
            pltpu.VMEM((SUBLANES, _RKV), F32),
            pltpu.VMEM((PAIRS, LANES, LANES), F32),
        ],
        compiler_params=_params(("arbitrary",)),
        name="rwkv",
    )(h, rkv, shift0, rkvprev0, s0,
      P["rw_mu_rkv"], P["rw_mu_wag"], P["rw_w0"], P["rw_w1"], P["rw_w2"], P["rw_a0"], P["rw_a1"],
      P["rw_a2"], P["rw_g1"], P["rw_g2"], P["rw_k_k"], P["rw_k_a"], P["rw_r_k"], P["rw_ln_w"],
      P["rw_ln_b"], C["e128"], C["cumtot%d" % tm])


def _sb_logs(z, mask):
    sp = jnp.log2(1.0 + jnp.exp2(-jnp.abs(z)))
    log_beta = jnp.minimum(z, 0.0) - sp
    log_rest = log_beta - z
    if mask is not None:
        log_rest = jnp.where(mask, log_rest, 0.0)
    return log_beta, log_rest


def _sb_weights(log_beta, log_rest, tail, carry, mask):
    wts = jnp.exp2(log_beta + tail + carry)
    if mask is not None:
        wts = jnp.where(mask, wts, 0.0)
    return wts.astype(BF16), carry + tail[:, 0:1] + log_rest[:, 0:1]


def _sb_body(*refs, tq, nsub, tkd, n_past, tkp):
    if n_past:
        q_ref, kc_ref, vc_ref, kp_ref, vp_ref, lowd_ref, lowp_ref, o_ref = refs
    else:
        q_ref, kc_ref, vc_ref, lowd_ref, o_ref = refs
    qi = pl.program_id(1)
    lane_lo = _lane_lo()
    rows = SB_HEADS * tq
    qs = [[_pair_stack(q_ref[s * tq:(s + 1) * tq, p * LANES:(p + 1) * LANES], lane_lo).astype(BF16)
           for p in range(PAIRS)] for s in range(nsub)]

    def block(ref, p, off, tk):
        cols = slice(None) if ref.shape[1] == tk else pl.ds(off, tk)
        return ref[p * LANES:(p + 1) * LANES, cols].astype(BF16)

    def step(k_ref, v_ref, off, tk, lower, st, masks):
        kbs = [block(k_ref, p, off, tk) for p in range(PAIRS)]
        vbs = [block(v_ref, p, off, tk) for p in range(PAIRS)]
        zs = [jnp.concatenate([_dot(qs[s][p], kbs[p]) for p in range(PAIRS)], axis=0) for s in range(nsub)]
        logs, tails = [], []
        for s in range(nsub):
            logs.append(_sb_logs(zs[s], masks[s]))
            tails.append(_dot_exact_rhs(logs[s][1], lower, terms=2))
        out = []
        for s in range(nsub):
            wts, carry = _sb_weights(logs[s][0], logs[s][1], tails[s], st[2 * s], masks[s])
            pv = jnp.concatenate(
                [_dot_nt(wts[2 * tq * p:2 * tq * (p + 1)], vbs[p]) for p in range(PAIRS)], axis=0)
            out += [carry, st[2 * s + 1] + pv]
        return tuple(out)

    lowd = lowd_ref[...]
    q0 = qi * (tq * nsub)
    jd = q0 // tkd
    offd = pl.multiple_of(jd * tkd, tkd)
    kpos = offd + lax.broadcasted_iota(jnp.int32, (rows, tkd), 1)
    rowq = lax.broadcasted_iota(jnp.int32, (rows, tkd), 0) & (tq - 1)
    masks = [kpos < (q0 + s * tq + rowq) for s in range(nsub)]
    st = (jnp.zeros((rows, 1), F32), jnp.zeros((rows, LANES), F32)) * nsub
    st = step(kc_ref, vc_ref, offd, tkd, lowd, st, masks)
    nomask = [None] * nsub

    def cur_step(jj, st):
        off = pl.multiple_of((jd - 1 - jj) * tkd, tkd)
        return step(kc_ref, vc_ref, off, tkd, lowd, st, nomask)

    st = lax.fori_loop(0, jd, cur_step, st)
    if n_past:
        lowp = lowp_ref[...]

        def past_step(jj, st):
            off = pl.multiple_of((n_past - 1 - jj) * tkp, tkp)
            return step(kp_ref, vp_ref, off, tkp, lowp, st, nomask)

        st = lax.fori_loop(0, n_past, past_step, st)
    for s in range(nsub):
        acc = st[2 * s + 1]
        for p in range(PAIRS):
            base = 2 * tq * p
            o_ref[s * tq:(s + 1) * tq, p * LANES:(p + 1) * LANES] = jnp.where(
                lane_lo, acc[base:base + tq], acc[base + tq:base + 2 * tq])


def _stick_breaking(q, kt, vt, kl, nseq, past_kt, past_vt, pl_, C):
    n = q.shape[0]
    t = n // nseq
    tq = min(SB_TQ, t)
    tkd = min(SB_TK, t)
    nsub = max(1, tkd // tq)
    nq = t // (tq * nsub)
    qmap = lambda b, i: (b * nq + i, 0)
    cmap = lambda b, i: (kl, b, 0, 0)
    in_specs = [
        pl.BlockSpec((tq * nsub, SB_DIM), qmap),
        pl.BlockSpec((None, None, SB_DIM, t), cmap),
        pl.BlockSpec((None, None, SB_DIM, t), cmap),
    ]
    args = [q, kt, vt]
    n_past = 0
    if past_kt is not None:
        plen = past_kt.shape[3]
        n_past = plen // SB_TK
        pmap = lambda b, i: (pl_, b, 0, 0)
        in_specs += [pl.BlockSpec((None, None, SB_DIM, plen), pmap), pl.BlockSpec((None, None, SB_DIM, plen), pmap)]
        args += [past_kt, past_vt]
    in_specs.append(pl.BlockSpec((tkd, tkd), lambda b, i: (0, 0)))
    args.append(C["low%d" % tkd])
    if n_past:
        in_specs.append(pl.BlockSpec((SB_TK, SB_TK), lambda b, i: (0, 0)))
        args.append(C["low%d" % SB_TK])
    return pl.pallas_call(
        functools.partial(_sb_body, tq=tq, nsub=nsub, tkd=tkd, n_past=n_past, tkp=SB_TK),
        grid=(nseq, nq),
        in_specs=in_specs,
        out_specs=pl.BlockSpec((tq * nsub, SB_DIM), qmap),
        out_shape=jax.ShapeDtypeStruct((n, SB_DIM), F32),
        compiler_params=_params(("arbitrary", "arbitrary")),
        name="stick_breaking",
    )(*args)


def _xattn_body(x_ref, orw_ref, osb_ref, ocv_ref, wout_ref, g_ref, wq_ref, qg_ref, mk_ref, mv_ref, wo_ref, o_ref):
    x = x_ref[...]
    x = x + _dot(orw_ref[...].astype(BF16), wout_ref[0:RW_DIM, :])
    x = x + _dot(osb_ref[...].astype(BF16), wout_ref[RW_DIM:RW_DIM + SB_DIM, :])
    x = x + _dot(ocv_ref[...].astype(BF16), wout_ref[RW_DIM + SB_DIM:D_MODEL, :])
    hx = _rms(x, g_ref[...]).astype(BF16)
    qf = _dot(hx, wq_ref[...])
    heads = []
    for hh in range(X_HEADS):
        cs = slice(hh * X_HEAD_DIM, (hh + 1) * X_HEAD_DIM)
        qh = _rms(qf[:, cs], qg_ref[...]).astype(BF16)
        s = _dot_nt(qh, mk_ref[:, cs].astype(BF16)) * (X_HEAD_DIM ** -0.5)
        s = jnp.exp(s - jnp.max(s, axis=-1, keepdims=True))
        attn = s / jnp.sum(s, axis=-1, keepdims=True)
        heads.append(_dot(attn.astype(BF16), mv_ref[:, cs].astype(BF16)))
    o = jnp.concatenate(heads, axis=1).astype(BF16)
    o_ref[...] = x + _dot(o, wo_ref[...])


def _xattn(x, orw, osb, ocv, l, nseq, P, mem_k, mem_v, mem_l):
    n = x.shape[0]
    t = n // nseq
    tq = min(XA_TQ, t)
    nq = t // tq
    nm = mem_k.shape[2]
    row = lambda b, i: (b * nq + i, 0)
    mmap = lambda b, i: (mem_l, b, 0, 0)
    return pl.pallas_call(
        _xattn_body,
        grid=(nseq, nq),
        in_specs=[
            pl.BlockSpec((tq, D_MODEL), row),
            pl.BlockSpec((tq, RW_DIM), row),
            pl.BlockSpec((tq, SB_DIM), row),
            pl.BlockSpec((tq, CONV_DIM), row),
            _lspec(l, (D_MODEL, D_MODEL)),
            _lspec(l, (1, D_MODEL)),
            _lspec(l, (D_MODEL, D_MODEL)),
            _lspec(l, (1, X_HEAD_DIM)),
            pl.BlockSpec((None, None, nm, D_MODEL), mmap),
            pl.BlockSpec((None, None, nm, D_MODEL), mmap),
            _lspec(l, (D_MODEL, D_MODEL)),
        ],
        out_specs=pl.BlockSpec((tq, D_MODEL), row),
        out_shape=jax.ShapeDtypeStruct((n, D_MODEL), F32),
        compiler_params=_params(("arbitrary", "arbitrary")),
        name="xattn",
    )(x, orw, osb, ocv, P["w_out"], P["x_norm"], P["x_wq"], P["x_q_norm"], mem_k, mem_v, P["x_wo"])


def _memkv_body(m_ref, g_ref, wk_ref, wv_ref, kg_ref, mk_ref, mv_ref):
    m = _rms(m_ref[...], g_ref[...]).astype(BF16)
    kf = _dot(m, wk_ref[...])
    for hh in range(X_HEADS):
        cs = slice(hh * X_HEAD_DIM, (hh + 1) * X_HEAD_DIM)
        mk_ref[:, cs] = _rms(kf[:, cs], kg_ref[...])
    mv_ref[...] = _dot(m, wv_ref[...])


def _memory_kv(mem, l, P):
    n = mem.shape[0]
    tm = min(MEM_TM, n)
    row = lambda i: (i, 0)
    return pl.pallas_call(
        _memkv_body,
        grid=(n // tm,),
        in_specs=[
            pl.BlockSpec((tm, D_MODEL), row),
            _lspec(l, (1, D_MODEL)),
            _lspec(l, (D_MODEL, D_MODEL)),
            _lspec(l, (D_MODEL, D_MODEL)),
            _lspec(l, (1, X_HEAD_DIM)),
        ],
        out_specs=[pl.BlockSpec((tm, D_MODEL), row), pl.BlockSpec((tm, D_MODEL), row)],
        out_shape=[jax.ShapeDtypeStruct((n, D_MODEL), F32), jax.ShapeDtypeStruct((n, D_MODEL), F32)],
        compiler_params=_params(("arbitrary",)),
        name="memory_kv",
    )(mem, P["mem_norm"], P["x_wk"], P["x_wv"], P["x_k_norm"])


def _layer(x, l, nseq, mem_k, mem_v, mem_l, past_kt, past_vt, s0, shift0, rkvprev0, conv0, kt_all, vt_all, P, C):
    n = x.shape[0]
    rs = n // nseq
    x = _ffn(x, l, P["ffn1_norm"], P["ffn1_wg"], P["ffn1_wu"], P["ffn1_wd"])
    h, rkv, q, o_cv, conv_buf, kt_all, vt_all = _mix_proj(x, l, rs, P, C, conv0, kt_all, vt_all)
    o_rw, s_t = _rwkv(h, rkv, l, rs, shift0, rkvprev0, s0, P, C)
    if kt_all.shape[1] == nseq:
        kt_cur, vt_cur, kl = kt_all, vt_all, l
    else:
        def per_seq(a):
            return a[l, 0].reshape(SB_DIM, nseq, rs).transpose(1, 0, 2)[None]
        kt_cur, vt_cur, kl = per_seq(kt_all), per_seq(vt_all), 0
    o_sb = _stick_breaking(q, kt_cur, vt_cur, kl, nseq, past_kt, past_vt, l, C)
    x = _xattn(x, o_rw, o_sb, o_cv, l, nseq, P, mem_k, mem_v, mem_l)
    x = _ffn(x, l, P["ffn2_norm"], P["ffn2_wg"], P["ffn2_wu"], P["ffn2_wd"])
    h_last = h.reshape(nseq, rs, D_MODEL)[:, rs - 1]
    return x, s_t, h_last, conv_buf, kt_all, vt_all


def _chunk_sum_matrix(tm):
    i = lax.broadcasted_iota(jnp.int32, (tm, tm), 0)
    j = lax.broadcasted_iota(jnp.int32, (tm, tm), 1)
    same = (i // CHUNK) == (j // CHUNK)
    return jnp.concatenate([same & (j <= i), same], axis=0).astype(BF16)


def _lower_ones(n, strict):
    i = lax.broadcasted_iota(jnp.int32, (n, n), 0)
    j = lax.broadcasted_iota(jnp.int32, (n, n), 1)
    return ((j < i) if strict else (j <= i)).astype(BF16)


def kernel(x_prompt, x_sample, mem_prompt, cache_sb_k, cache_sb_v, state_rwkv, state_shift, state_conv,
           cache_mem_k, cache_mem_v, ffn1_norm, ffn1_wg, ffn1_wu, ffn1_wd, mix_norm, w_in, w_out, rw_mu_rkv,
           rw_mu_wag, rw_w0, rw_w1, rw_w2, rw_a0, rw_a1, rw_a2, rw_g1, rw_g2, rw_k_k, rw_k_a, rw_r_k, rw_ln_w,
           rw_ln_b, sb_q_norm, sb_k_norm, cv_w, x_norm, mem_norm, x_wq, x_wk, x_wv, x_wo, x_q_norm, x_k_norm,
           ffn2_norm, ffn2_wg, ffn2_wu, ffn2_wd):
    depth = w_in.shape[0]
    bp, tp, _ = x_prompt.shape
    bs, ts, _ = x_sample.shape
    n_mem = mem_prompt.shape[1]
    plen = cache_sb_k.shape[2]

    def vec(a):
        return a.reshape(a.shape[0], 1, -1)

    def per_head(a):
        return jnp.tile(a, (1, SB_HEADS))

    w_in_b = w_in.astype(BF16)
    P = dict(
        ffn1_norm=vec(ffn1_norm), ffn1_wg=ffn1_wg.astype(BF16), ffn1_wu=ffn1_wu.astype(BF16),
        ffn1_wd=ffn1_wd.astype(BF16),
        mix_norm=vec(mix_norm), w_rq=w_in_b[:, :, 0:_K0],
        w_kvt=jnp.swapaxes(w_in_b[:, :, _K0:_C0], 1, 2), w_cv=w_in_b[:, :, _C0:],
        w_out=w_out.astype(BF16),
        rw_mu_rkv=rw_mu_rkv.reshape(depth, 1, _RKV), rw_mu_wag=rw_mu_wag,
        rw_w0=vec(rw_w0), rw_w1=rw_w1.astype(BF16), rw_w2=rw_w2.astype(BF16),
        rw_a0=vec(rw_a0), rw_a1=rw_a1.astype(BF16), rw_a2=rw_a2.astype(BF16),
        rw_g1=rw_g1.astype(BF16), rw_g2=rw_g2.astype(BF16),
        rw_k_k=vec(rw_k_k), rw_k_a=vec(rw_k_a), rw_r_k=rw_r_k.reshape(depth, 1, RW_DIM),
        rw_ln_w=vec(rw_ln_w), rw_ln_b=vec(rw_ln_b),
        sb_q_norm=vec(per_head(sb_q_norm)), sb_k_norm_col=per_head(sb_k_norm).reshape(depth, SB_DIM, 1), cv_w=cv_w,
        x_norm=vec(x_norm), mem_norm=vec(mem_norm),
        x_wq=x_wq.astype(BF16), x_wk=x_wk.astype(BF16), x_wv=x_wv.astype(BF16), x_wo=x_wo.astype(BF16),
        x_q_norm=vec(x_q_norm), x_k_norm=vec(x_k_norm),
        ffn2_norm=vec(ffn2_norm), ffn2_wg=ffn2_wg.astype(BF16), ffn2_wu=ffn2_wu.astype(BF16),
        ffn2_wd=ffn2_wd.astype(BF16),
    )
    hid = lax.broadcasted_iota(jnp.int32, (LANES, LANES), 0) // HEAD_DIM
    hjd = lax.broadcasted_iota(jnp.int32, (LANES, LANES), 1) // HEAD_DIM
    C = {"e128": (hid == hjd).astype(BF16)}
    for tm in {min(RWKV_TM, bp * tp), min(RWKV_TM, bs * ts)}:
        C["cumtot%d" % tm] = _chunk_sum_matrix(tm)
    for size in {min(SB_TK, tp), min(SB_TK, ts), SB_TK}:
        C["low%d" % size] = _lower_ones(size, True)

    dt = x_prompt.dtype
    zero_s = jnp.zeros((bp, RW_HEADS, HEAD_DIM, HEAD_DIM), state_rwkv.dtype)
    zero_shift = jnp.zeros((bp, 1, D_MODEL), dt)
    zero_rkv = jnp.zeros((bp, 1, _RKV), dt)
    zero_conv = jnp.zeros((bp, CONV_W - 1, CONV_DIM), dt)
    past_kt = jnp.transpose(cache_sb_k, (0, 1, 3, 4, 2)).reshape(depth, bs, SB_DIM, plen)
    past_vt = jnp.transpose(cache_sb_v, (0, 1, 3, 4, 2)).reshape(depth, bs, SB_DIM, plen)
    cmk = cache_mem_k.reshape(depth, bs, n_mem, D_MODEL)
    cmv = cache_mem_v.reshape(depth, bs, n_mem, D_MODEL)

    yp = x_prompt.reshape(bp * tp, D_MODEL)
    ys = x_sample.reshape(bs * ts, D_MODEL)
    mem = mem_prompt.reshape(bp * n_mem, D_MODEL)
    pkt = jnp.zeros((depth, bp, SB_DIM, tp), dt)
    pvt = jnp.zeros((depth, bp, SB_DIM, tp), dt)
    skt = jnp.zeros((depth, 1, SB_DIM, bs * ts), dt)
    svt = jnp.zeros((depth, 1, SB_DIM, bs * ts), dt)
    pS, psh, pcv, pmk, pmv = [], [], [], [], []
    sS, ssh, scv = [], [], []
    for l in range(depth):
        mk, mv = _memory_kv(mem, l, P)
        mk4 = mk.reshape(1, bp, n_mem, D_MODEL)
        mv4 = mv.reshape(1, bp, n_mem, D_MODEL)
        yp, s_t, sh, cbuf, pkt, pvt = _layer(yp, l, bp, mk4, mv4, 0, None, None, zero_s, zero_shift, zero_rkv,
                                             zero_conv, pkt, pvt, P, C)
        pS.append(s_t); psh.append(sh); pcv.append(cbuf); pmk.append(mk); pmv.append(mv)
        shift_l = state_shift[l]
        rkvprev = _state_proj(shift_l, l, P["w_rq"]).reshape(bs, 1, _RKV)
        ys, s_t, sh, cbuf, skt, svt = _layer(ys, l, bs, cmk, cmv, l, past_kt, past_vt, state_rwkv[l],
                                             shift_l.reshape(bs, 1, D_MODEL), rkvprev, state_conv[l], skt, svt, P, C)
        sS.append(s_t); ssh.append(sh); scv.append(cbuf)

    def stack(xs, shape):
        return jnp.stack(xs).reshape((depth,) + shape)

    def prompt_kv(a):
        return jnp.transpose(a.reshape(depth, bp, SB_HEADS, HEAD_DIM, tp), (0, 1, 4, 2, 3))

    def sample_kv(a):
        return jnp.transpose(a.reshape(depth, SB_HEADS, HEAD_DIM, bs, ts), (0, 3, 4, 1, 2))

    return (
        yp.reshape(bp, tp, D_MODEL), ys.reshape(bs, ts, D_MODEL),
        prompt_kv(pkt), prompt_kv(pvt),
        stack(pS, (bp, RW_HEADS, HEAD_DIM, HEAD_DIM)), stack(psh, (bp, D_MODEL)),
        stack(pcv, (bp, CONV_W - 1, CONV_DIM)),
        stack(pmk, (bp, n_mem, X_HEADS, X_HEAD_DIM)), stack(pmv, (bp, n_mem, X_HEADS, X_HEAD_DIM)),
        sample_kv(skt), sample_kv(svt),
        stack(sS, (bs, RW_HEADS, HEAD_DIM, HEAD_DIM)), stack(ssh, (bs, D_MODEL)),
        stack(scv, (bs, CONV_W - 1, CONV_DIM)),
    )
```

```python
import functools

import jax
import jax.numpy as jnp
from jax import lax
from jax.experimental import pallas as pl
from jax.experimental.pallas import tpu as pltpu

F32 = jnp.float32
BF16 = jnp.bfloat16

D_MODEL = 1024
HEAD_DIM = 64
RW_HEADS = 6
SB_HEADS = 6
RW_DIM = RW_HEADS * HEAD_DIM
SB_DIM = SB_HEADS * HEAD_DIM
CONV_DIM = 256
CONV_W = 3
FFN_DIM = 2816
X_HEADS = 4
X_HEAD_DIM = D_MODEL // X_HEADS
RMS_EPS = 1e-6
GN_EPS = 64e-5
L2_EPS = 1e-12
LOG2E = 1.4426950408889634

CHUNK = 64
LANES = 128
SUBLANES = 8
PAIRS = RW_DIM // LANES
VMEM_LIMIT = 56 * 1024 * 1024

FFN_TM = 1024
FFN_TF = 256
MIX_TM = 512
RWKV_TM = 256
SB_TQ = 128
SB_TK = 256
SB_CHAINS = 2
XA_TQ = 512
MEM_TM = 512

_RKV = 3 * RW_DIM
_Q0 = _RKV
_K0 = _Q0 + SB_DIM
_V0 = _K0 + SB_DIM
_C0 = _V0 + SB_DIM


def _dot(a, b):
    return jnp.dot(a, b, preferred_element_type=F32)


def _dot_nt(a, b):
    return lax.dot_general(a, b, (((1,), (1,)), ((), ())), preferred_element_type=F32)


def _dot_tn(a, b):
    return lax.dot_general(a, b, (((0,), (0,)), ((), ())), preferred_element_type=F32)


def _split(x, terms):
    parts = []
    rem = x
    for t in range(terms):
        p = rem.astype(BF16)
        parts.append(p)
        if t + 1 < terms:
            rem = rem - p.astype(F32)
    return parts


def _dot_exact_rhs(x, m, terms=3):
    acc = None
    for p in _split(x, terms):
        y = _dot(p, m)
        acc = y if acc is None else acc + y
    return acc


def _dot_exact_lhs(m, x, terms=3):
    acc = None
    for p in _split(x, terms):
        y = _dot(m, p)
        acc = y if acc is None else acc + y
    return acc


def _head_sums(x, e_pair, terms=2):
    outs = []
    for p in range(x.shape[1] // LANES):
        outs.append(_dot_exact_rhs(x[:, p * LANES:(p + 1) * LANES], e_pair, terms))
    return jnp.concatenate(outs, axis=1)


def _rms(x, g):
    return x * lax.rsqrt(jnp.mean(x * x, axis=-1, keepdims=True) + RMS_EPS) * g


def _sigmoid(x):
    return 1.0 / (1.0 + jnp.exp(-x))


def _softplus(y):
    return jnp.maximum(y, 0.0) + jnp.log1p(jnp.exp(-jnp.abs(y)))


def _params(sem):
    return pltpu.CompilerParams(dimension_semantics=sem, vmem_limit_bytes=VMEM_LIMIT)


def _lspec(l, tail):
    nz = (0,) * len(tail)
    return pl.BlockSpec((None,) + tuple(tail), lambda *_: (l,) + nz)


def _lane_lo():
    return lax.broadcasted_iota(jnp.int32, (1, LANES), 1) < HEAD_DIM


def _pair_stack(xp, lane_lo):
    return jnp.concatenate([jnp.where(lane_lo, xp, 0.0), jnp.where(lane_lo, 0.0, xp)], axis=0)


def _ffn_body(x_ref, g_ref, wg_ref, wu_ref, wd_ref, o_ref):
    x = x_ref[...]
    h = _rms(x, g_ref[...]).astype(BF16)
    acc = None
    for f in range(FFN_DIM // FFN_TF):
        fs = slice(f * FFN_TF, (f + 1) * FFN_TF)
        gate = _dot(h, wg_ref[:, fs])
        up = _dot(h, wu_ref[:, fs])
        act = (gate * _sigmoid(gate) * up).astype(BF16)
        part = _dot(act, wd_ref[fs, :])
        acc = part if acc is None else acc + part
    o_ref[...] = x + 0.5 * acc


def _ffn(x, l, norm, wg, wu, wd):
    n = x.shape[0]
    tm = min(FFN_TM, n)
    once = pl.Buffered(1)
    return pl.pallas_call(
        _ffn_body,
        grid=(n // tm,),
        in_specs=[
            pl.BlockSpec((tm, D_MODEL), lambda i: (i, 0)),
            _lspec(l, (1, D_MODEL)),
            pl.BlockSpec((None, D_MODEL, FFN_DIM), lambda i: (l, 0, 0), pipeline_mode=once),
            pl.BlockSpec((None, D_MODEL, FFN_DIM), lambda i: (l, 0, 0), pipeline_mode=once),
            pl.BlockSpec((None, FFN_DIM, D_MODEL), lambda i: (l, 0, 0), pipeline_mode=once),
        ],
        out_specs=pl.BlockSpec((tm, D_MODEL), lambda i: (i, 0)),
        out_shape=jax.ShapeDtypeStruct((n, D_MODEL), F32),
        compiler_params=_params(("arbitrary",)),
        name="ffn",
    )(x, norm, wg, wu, wd)


def _mix_body(x_ref, g_ref, wrq_ref, wkv_ref, wcv_ref, qg_ref, kg_ref, e_ref, cw_ref, c0_ref, kin_ref, vin_ref,
              h_ref, rkv_ref, q_ref, ocv_ref, cbuf_ref, kt_ref, vt_ref, tail_scr, *, tm, rs):
    del kin_ref, vin_ref
    i = pl.program_id(0)
    h = _rms(x_ref[...], g_ref[...])
    h_ref[...] = h
    hb = h.astype(BF16)
    rkv_ref[...] = _dot(hb, wrq_ref[:, 0:_RKV])
    sq = _dot(hb, wrq_ref[:, _RKV:_RKV + SB_DIM])
    ms = _head_sums(sq * sq, e_ref[...], terms=3) * (1.0 / HEAD_DIM)
    q_ref[...] = sq * lax.rsqrt(ms + RMS_EPS) * qg_ref[...] * (HEAD_DIM ** -0.5 * LOG2E)

    kt = _dot_nt(wkv_ref[0:SB_DIM, :], hb)
    k3 = kt.reshape(SB_HEADS, HEAD_DIM, tm)
    kms = jnp.mean(k3 * k3, axis=1, keepdims=True)
    kt_ref[...] = (k3 * lax.rsqrt(kms + RMS_EPS)).reshape(SB_DIM, tm) * kg_ref[...]
    vt_ref[...] = _dot_nt(wkv_ref[SB_DIM:2 * SB_DIM, :], hb)

    cb = _dot(hb, wcv_ref[:, 0:CONV_DIM])
    u = _dot(hb, wcv_ref[:, CONV_DIM:2 * CONV_DIM]) * _dot(hb, wcv_ref[:, 2 * CONV_DIM:3 * CONV_DIM])
    w0 = cw_ref[0:1, :]
    w1 = cw_ref[1:2, :]
    w2 = cw_ref[2:3, :]
    seg = min(tm, rs)
    nseg = tm // seg
    if rs >= tm:
        tiles_per_seq = rs // tm

        @pl.when(i % tiles_per_seq == 0)
        def _():
            tail_scr[SUBLANES - 2:SUBLANES, :] = c0_ref[0]

    row = lax.broadcasted_iota(jnp.int32, (seg, CONV_DIM), 0)
    for s in range(nseg):
        us = u[s * seg:(s + 1) * seg]
        if rs >= tm:
            b0 = tail_scr[SUBLANES - 2:SUBLANES - 1, :]
            b1 = tail_scr[SUBLANES - 1:SUBLANES, :]
        else:
            b0 = c0_ref[s, 0:1, :]
            b1 = c0_ref[s, 1:2, :]
        p1 = jnp.where(row == 0, b1, pltpu.roll(us, 1, 0))
        p2 = jnp.where(row == 0, b0, jnp.where(row == 1, b1, pltpu.roll(us, 2, 0)))
        y = p2 * w0 + p1 * w1 + us * w2
        ocv_ref[s * seg:(s + 1) * seg, :] = cb[s * seg:(s + 1) * seg] * y
        tail_scr[...] = us[seg - SUBLANES:seg]
        cbuf_ref[s] = tail_scr[SUBLANES - 2:SUBLANES, :]


def _mix_proj(x, l, rs, P, C, conv0, kt_all, vt_all):
    n = x.shape[0]
    tm = min(MIX_TM, n)
    nseq = n // rs
    if rs >= tm:
        tps = rs // tm
        nsb = 1
        smap = lambda i: (i // tps, 0, 0)
        tmap = lambda i: (l, i // tps, 0, i % tps)
    else:
        nsb = tm // rs
        smap = lambda i: (i, 0, 0)
        tmap = lambda i: (l, 0, 0, i)
    row = lambda i: (i, 0)
    tspec = pl.BlockSpec((None, None, SB_DIM, tm), tmap)
    return pl.pallas_call(
        functools.partial(_mix_body, tm=tm, rs=rs),
        grid=(n // tm,),
        in_specs=[
            pl.BlockSpec((tm, D_MODEL), row),
            _lspec(l, (1, D_MODEL)),
            _lspec(l, (D_MODEL, _RKV + SB_DIM)),
            _lspec(l, (2 * SB_DIM, D_MODEL)),
            _lspec(l, (D_MODEL, 3 * CONV_DIM)),
            _lspec(l, (1, SB_DIM)),
            _lspec(l, (SB_DIM, 1)),
            pl.BlockSpec((LANES, LANES), lambda i: (0, 0)),
            _lspec(l, (CONV_W, CONV_DIM)),
            pl.BlockSpec((nsb, CONV_W - 1, CONV_DIM), smap),
            pl.BlockSpec(memory_space=pl.ANY),
            pl.BlockSpec(memory_space=pl.ANY),
        ],
        out_specs=[
            pl.BlockSpec((tm, D_MODEL), row),
            pl.BlockSpec((tm, _RKV), row),
            pl.BlockSpec((tm, SB_DIM), row),
            pl.BlockSpec((tm, CONV_DIM), row),
            pl.BlockSpec((nsb, CONV_W - 1, CONV_DIM), smap),
            tspec,
            tspec,
        ],
        out_shape=[
            jax.ShapeDtypeStruct((n, D_MODEL), F32),
            jax.ShapeDtypeStruct((n, _RKV), F32),
            jax.ShapeDtypeStruct((n, SB_DIM), F32),
            jax.ShapeDtypeStruct((n, CONV_DIM), F32),
            jax.ShapeDtypeStruct((nseq, CONV_W - 1, CONV_DIM), F32),
            jax.ShapeDtypeStruct(kt_all.shape, F32),
            jax.ShapeDtypeStruct(vt_all.shape, F32),
        ],
        input_output_aliases={10: 5, 11: 6},
        scratch_shapes=[pltpu.VMEM((SUBLANES, CONV_DIM), F32)],
        compiler_params=_params(("arbitrary",)),
        name="mix_proj",
    )(x, P["mix_norm"], P["w_rq"], P["w_kvt"], P["w_cv"], P["sb_q_norm"], P["sb_k_norm_col"], C["e128"],
      P["cv_w"], conv0, kt_all, vt_all)


def _state_proj_body(s_ref, w_ref, o_ref):
    o_ref[...] = _dot(s_ref[...].astype(BF16), w_ref[:, 0:_RKV])


def _state_proj(shift0, l, w_rq):
    b = shift0.shape[0]
    return pl.pallas_call(
        _state_proj_body,
        grid=(1,),
        in_specs=[pl.BlockSpec((b, D_MODEL), lambda i: (0, 0)), _lspec(l, (D_MODEL, _RKV + SB_DIM))],
        out_specs=pl.BlockSpec((b, _RKV), lambda i: (0, 0)),
        out_shape=jax.ShapeDtypeStruct((b, _RKV), F32),
        compiler_params=_params(("arbitrary",)),
        name="state_proj",
    )(shift0, w_rq)


def _unit_lower_inverses(lmats):
    n = lmats[0].shape[0]
    eye = (lax.broadcasted_iota(jnp.int32, (n, n), 0) == lax.broadcasted_iota(jnp.int32, (n, n), 1)).astype(F32)
    ps = [eye + lm for lm in lmats]
    lks = []
    for lm in lmats:
        lb = lm.astype(BF16)
        lks.append(_dot(lb, lb))
    power = 2
    while power < CHUNK:
        for t in range(len(lmats)):
            lkb = lks[t].astype(BF16)
            if 2 * power < CHUNK:
                both = _dot(lkb, jnp.concatenate([lks[t], ps[t]], axis=1).astype(BF16))
                lks[t] = both[:, 0:n]
                ps[t] = ps[t] + both[:, n:2 * n]
            else:
                ps[t] = ps[t] + _dot(lkb, ps[t].astype(BF16))
        power *= 2
    return ps


def _state_to_pairs(s_ref, sq):
    zero = jnp.zeros((HEAD_DIM, HEAD_DIM), F32)
    out = []
    for p in range(PAIRS):
        top = jnp.concatenate([s_ref[sq, 2 * p], zero], axis=1)
        bot = jnp.concatenate([zero, s_ref[sq, 2 * p + 1]], axis=1)
        out.append(jnp.concatenate([top, bot], axis=0))
    return out


def _pairs_to_state(sbd, sout_ref, sq):
    for p in range(PAIRS):
        sout_ref[sq, 2 * p] = sbd[p][0:HEAD_DIM, 0:HEAD_DIM]
        sout_ref[sq, 2 * p + 1] = sbd[p][HEAD_DIM:LANES, HEAD_DIM:LANES]


def _rwkv_body(h_ref, rkv_ref, sh0_ref, rp0_ref, s0_ref,
               murkv_ref, muwag_ref, w0_ref, w1_ref, w2_ref, a0_ref, a1_ref, a2_ref, g1_ref, g2_ref,
               kk_ref, ka_ref, rk_ref, lnw_ref, lnb_ref, e_ref, csum_ref,
               o_ref, sout_ref, hl_scr, rl_scr, s_scr, *, tm, rs):
    i = pl.program_id(0)
    seg = min(tm, rs)
    nseg = tm // seg
    nchunk = tm // CHUNK
    carried = rs >= tm
    if carried:
        tiles_per_seq = rs // tm

        @pl.when(i % tiles_per_seq == 0)
        def _():
            hl_scr[SUBLANES - 1:SUBLANES, :] = sh0_ref[0]
            rl_scr[SUBLANES - 1:SUBLANES, :] = rp0_ref[0]
            init = _state_to_pairs(s0_ref, 0)
            for p in range(PAIRS):
                s_scr[p] = init[p]

    h = h_ref[...]
    rkv = rkv_ref[...]
    if carried:
        hprev_rows = hl_scr[SUBLANES - 1:SUBLANES, :]
        rprev_rows = rl_scr[SUBLANES - 1:SUBLANES, :]
    else:
        hprev_rows = jnp.concatenate(
            [jnp.broadcast_to(sh0_ref[s], (seg, D_MODEL)) for s in range(nseg)], axis=0)
        rprev_rows = jnp.concatenate(
            [jnp.broadcast_to(rp0_ref[s], (seg, _RKV)) for s in range(nseg)], axis=0)
    first_h = (lax.broadcasted_iota(jnp.int32, (tm, D_MODEL), 0) & (seg - 1)) == 0
    first_r = (lax.broadcasted_iota(jnp.int32, (tm, _RKV), 0) & (seg - 1)) == 0
    hp = jnp.where(first_h, hprev_rows, pltpu.roll(h, 1, 0))
    rp = jnp.where(first_r, rprev_rows, pltpu.roll(rkv, 1, 0))
    if carried:
        hl_scr[...] = h[tm - SUBLANES:tm]
        rl_scr[...] = rkv[tm - SUBLANES:tm]

    rkv = rkv + murkv_ref[...] * (rp - rkv)
    r = rkv[:, 0:RW_DIM]
    k = rkv[:, RW_DIM:2 * RW_DIM]
    v = rkv[:, 2 * RW_DIM:3 * RW_DIM]
    dx = hp - h
    xw = (h + muwag_ref[0:1, :] * dx).astype(BF16)
    xa = (h + muwag_ref[1:2, :] * dx).astype(BF16)
    xg = (h + muwag_ref[2:3, :] * dx).astype(BF16)
    wl = w0_ref[...] + _dot(jnp.tanh(_dot(xw, w1_ref[...])).astype(BF16), w2_ref[...])
    w = -_softplus(-wl) - 0.5
    logdecay = -jnp.exp(w)
    a = _sigmoid(a0_ref[...] + _dot(_dot(xa, a1_ref[...]).astype(BF16), a2_ref[...]))
    gate = _dot(_sigmoid(_dot(xg, g1_ref[...])).astype(BF16), g2_ref[...])
    e = e_ref[...]
    kk = k * kk_ref[...]
    kk = kk * lax.rsqrt(_head_sums(kk * kk, e) + L2_EPS)
    k = k * (1.0 + (a - 1.0) * ka_ref[...])
    b = kk * a
    bonus = _head_sums(r * k * rk_ref[...], e) * v

    ci = lax.broadcasted_iota(jnp.int32, (LANES, LANES), 0)
    cj = lax.broadcasted_iota(jnp.int32, (LANES, LANES), 1)
    strict = cj < ci
    incl = cj <= ci
    lane_lo = _lane_lo()

    sums = _dot_exact_lhs(csum_ref[...], logdecay)
    cum = sums[0:tm]
    tot = sums[tm:2 * tm]
    g_incl = jnp.exp(cum)
    g_prev = jnp.exp(cum - logdecay)
    g_inv = jnp.exp(-cum)
    g_rem = jnp.exp(tot - cum)
    g_tot = jnp.exp(tot)
    rq = r * g_incl
    kq = k * g_inv
    bq = b * g_inv
    aq = -kk * g_prev
    kz = k * g_rem
    bz = b * g_rem
    items = [(c, p) for c in range(nchunk) for p in range(PAIRS)]

    def stacked(x, c, p):
        return _pair_stack(x[c * CHUNK:(c + 1) * CHUNK, p * LANES:(p + 1) * LANES], lane_lo).astype(BF16)

    work = [dict(aq=stacked(aq, c, p), rq=stacked(rq, c, p), v=stacked(v, c, p),
                 kzbz=jnp.concatenate([stacked(kz, c, p), stacked(bz, c, p)], axis=0),
                 gtot=g_tot[c * CHUNK:c * CHUNK + 1, p * LANES:(p + 1) * LANES]) for c, p in items]
    quads = [_dot_nt(jnp.concatenate([wk["aq"], wk["rq"]], axis=0),
                     jnp.concatenate([stacked(kq, c, p), stacked(bq, c, p)], axis=0))
             for wk, (c, p) in zip(work, items)]
    lmats = []
    for wk, quad in zip(work, quads):
        wk["m_ak"] = jnp.where(strict, quad[0:LANES, 0:LANES], 0.0).astype(BF16)
        lmats.append(jnp.where(strict, quad[0:LANES, LANES:2 * LANES], 0.0))
        wk["mr"] = jnp.concatenate([jnp.where(incl, quad[LANES:2 * LANES, 0:LANES], 0.0),
                                    jnp.where(incl, quad[LANES:2 * LANES, LANES:2 * LANES], 0.0)],
                                   axis=1).astype(BF16)
    for wk in work:
        wk["x1"] = _dot(wk["m_ak"], wk["v"])
    tinvs = _unit_lower_inverses(lmats)
    for wk, tinv in zip(work, tinvs):
        wu = _dot(tinv.astype(BF16), jnp.concatenate([wk["aq"], wk["x1"].astype(BF16)], axis=1))
        wk["wr"] = jnp.concatenate([wu[:, 0:LANES].astype(BF16), wk["rq"]], axis=0)
        wk["u0"] = wu[:, LANES:2 * LANES]

    if carried:
        sbd = [s_scr[p] for p in range(PAIRS)]
    o_chunks = []
    for c in range(nchunk):
        sq = (c * CHUNK) // seg
        if not carried:
            sbd = _state_to_pairs(s0_ref, sq)
        wks = work[c * PAIRS:(c + 1) * PAIRS]
        ys = [_dot_nt(wks[p]["wr"], sbd[p].astype(BF16)) for p in range(PAIRS)]
        vus = [jnp.concatenate([wks[p]["v"], (ys[p][0:LANES] + wks[p]["u0"]).astype(BF16)], axis=0)
               for p in range(PAIRS)]
        sbd = [sbd[p] * wks[p]["gtot"] + _dot_tn(vus[p], wks[p]["kzbz"]) for p in range(PAIRS)]
        o_ss = [ys[p][LANES:2 * LANES] + _dot(wks[p]["mr"], vus[p]) for p in range(PAIRS)]
        if not carried:
            _pairs_to_state(sbd, sout_ref, sq)
        o_chunks.append(jnp.concatenate([o_s[0:CHUNK] + o_s[CHUNK:2 * CHUNK] for o_s in o_ss], axis=1))
    if carried:
        for p in range(PAIRS):
            s_scr[p] = sbd[p]
        _pairs_to_state(sbd, sout_ref, 0)
    o = jnp.concatenate(o_chunks, axis=0)
    mean = _head_sums(o, e) * (1.0 / HEAD_DIM)
    d = o - mean
    var = _head_sums(d * d, e) * (1.0 / HEAD_DIM)
    on = d * lax.rsqrt(var + GN_EPS) * lnw_ref[...] + lnb_ref[...]
    o_ref[...] = (on + bonus) * gate


def _rwkv(h, rkv, l, rs, shift0, rkvprev0, s0, P, C):
    n = h.shape[0]
    tm = min(RWKV_TM, n)
    nseq = n // rs
    if rs >= tm:
        tps = rs // tm
        nsb = 1
        smap3 = lambda i: (i // tps, 0, 0)
        smap4 = lambda i: (i // tps, 0, 0, 0)
    else:
        nsb = tm // rs
        smap3 = lambda i: (i, 0, 0)
        smap4 = lambda i: (i, 0, 0, 0)
    row = lambda i: (i, 0)
    lora_w, lora_a, lora_g = P["rw_w1"].shape[-1], P["rw_a1"].shape[-1], P["rw_g1"].shape[-1]
    return pl.pallas_call(
        functools.partial(_rwkv_body, tm=tm, rs=rs),
        grid=(n // tm,),
        in_specs=[
            pl.BlockSpec((tm, D_MODEL), row),
            pl.BlockSpec((tm, _RKV), row),
            pl.BlockSpec((nsb, 1, D_MODEL), smap3),
            pl.BlockSpec((nsb, 1, _RKV), smap3),
            pl.BlockSpec((nsb, RW_HEADS, HEAD_DIM, HEAD_DIM), smap4),
            _lspec(l, (1, _RKV)),
            _lspec(l, (3, D_MODEL)),
            _lspec(l, (1, RW_DIM)),
            _lspec(l, (D_MODEL, lora_w)),
            _lspec(l, (lora_w, RW_DIM)),
            _lspec(l, (1, RW_DIM)),
            _lspec(l, (D_MODEL, lora_a)),
            _lspec(l, (lora_a, RW_DIM)),
            _lspec(l, (D_MODEL, lora_g)),
            _lspec(l, (lora_g, RW_DIM)),
            _lspec(l, (1, RW_DIM)),
            _lspec(l, (1, RW_DIM)),
            _lspec(l, (1, RW_DIM)),
            _lspec(l, (1, RW_DIM)),
            _lspec(l, (1, RW_DIM)),
            pl.BlockSpec((LANES, LANES), lambda i: (0, 0)),
            pl.BlockSpec((2 * tm, tm), lambda i: (0, 0)),
        ],
        out_specs=[
            pl.BlockSpec((tm, RW_DIM), row),
            pl.BlockSpec((nsb, RW_HEADS, HEAD_DIM, HEAD_DIM), smap4),
        ],
        out_shape=[
            jax.ShapeDtypeStruct((n, RW_DIM), F32),
            jax.ShapeDtypeStruct((nseq, RW_HEADS, HEAD_DIM, HEAD_DIM), F32),
        ],
        scratch_shapes=[
            pltpu.VMEM((SUBLANES, D_MODEL), F32),
            pltpu.VMEM((SUBLANES, _RKV), F32),
            pltpu.VMEM((PAIRS, LANES, LANES), F32),
        ],
        compiler_params=_params(("arbitrary",)),
        name="rwkv",
    )(h, rkv, shift0, rkvprev0, s0,
      P["rw_mu_rkv"], P["rw_mu_wag"], P["rw_w0"], P["rw_w1"], P["rw_w2"], P["rw_a0"], P["rw_a1"],
      P["rw_a2"], P["rw_g1"], P["rw_g2"], P["rw_k_k"], P["rw_k_a"], P["rw_r_k"], P["rw_ln_w"],
      P["rw_ln_b"], C["e128"], C["cumtot%d" % tm])


def _sb_logs(z, mask):
    neg_abs = lax.bitcast_convert_type(lax.bitcast_convert_type(z, jnp.int32) | jnp.int32(-2 ** 31), F32)
    sp = jnp.log2(1.0 + jnp.exp2(neg_abs))
    log_beta = jnp.minimum(z, 0.0) - sp
    log_rest = log_beta - z
    if mask is not None:
        log_rest = jnp.where(mask, log_rest, 0.0)
    return log_beta, log_rest


def _sb_weights(log_beta, log_rest, tail, carry, mask):
    wts = jnp.exp2(log_beta + tail + carry)
    if mask is not None:
        wts = jnp.where(mask, wts, 0.0)
    return wts.astype(BF16), carry + tail[:, 0:1] + log_rest[:, 0:1]


def _sb_body(*refs, tq, nsub, nseqs, tkd, n_past, tkp):
    if n_past:
        q_ref, kc_ref, vc_ref, kp_ref, vp_ref, lowd_ref, lowp_ref, o_ref = refs
    else:
        q_ref, kc_ref, vc_ref, lowd_ref, o_ref = refs
    qi = pl.program_id(1)
    lane_lo = _lane_lo()
    rows = SB_HEADS * tq
    chains = [(g, s) for g in range(nseqs) for s in range(nsub)]
    qs = [[_pair_stack(q_ref[c * tq:(c + 1) * tq, p * LANES:(p + 1) * LANES], lane_lo).astype(BF16)
           for p in range(PAIRS)] for c in range(len(chains))]

    def block(ref, g, p, off, tk):
        cols = slice(None) if ref.shape[2] == tk else pl.ds(off, tk)
        return ref[g, p * LANES:(p + 1) * LANES, cols].astype(BF16)

    def step(k_ref, v_ref, off, tk, lower, st, masks):
        kbs = [[block(k_ref, g, p, off, tk) for p in range(PAIRS)] for g in range(nseqs)]
        vbs = [[block(v_ref, g, p, off, tk) for p in range(PAIRS)] for g in range(nseqs)]
        zs = [jnp.concatenate([_dot(qs[c][p], kbs[g][p]) for p in range(PAIRS)], axis=0)
              for c, (g, _) in enumerate(chains)]
        logs, tails = [], []
        for c in range(len(chains)):
            logs.append(_sb_logs(zs[c], masks[c]))
            hi, lo = _split(logs[c][1], 2)
            tails.append(_dot(jnp.concatenate([hi, lo], axis=1), lower))
        out = []
        for c, (g, _) in enumerate(chains):
            wts, carry = _sb_weights(logs[c][0], logs[c][1], tails[c], st[2 * c], masks[c])
            pv = jnp.concatenate(
                [_dot_nt(wts[2 * tq * p:2 * tq * (p + 1)], vbs[g][p]) for p in range(PAIRS)], axis=0)
            out += [carry, st[2 * c + 1] + pv]
        return tuple(out)

    lowd = lowd_ref[...]
    q0 = qi * (tq * nsub)
    jd = q0 // tkd
    offd = pl.multiple_of(jd * tkd, tkd)
    kpos = offd + lax.broadcasted_iota(jnp.int32, (rows, tkd), 1)
    rowq = lax.broadcasted_iota(jnp.int32, (rows, tkd), 0) & (tq - 1)
    masks = [kpos < (q0 + s * tq + rowq) for _, s in chains]
    st = (jnp.zeros((rows, 1), F32), jnp.zeros((rows, LANES), F32)) * len(chains)
    st = step(kc_ref, vc_ref, offd, tkd, lowd, st, masks)
    nomask = [None] * len(chains)

    def cur_step(jj, st):
        off = pl.multiple_of((jd - 1 - jj) * tkd, tkd)
        return step(kc_ref, vc_ref, off, tkd, lowd, st, nomask)

    st = lax.fori_loop(0, jd, cur_step, st)
    if n_past:
        lowp = lowp_ref[...]

        def past_step(jj, st):
            off = pl.multiple_of((n_past - 1 - jj) * tkp, tkp)
            return step(kp_ref, vp_ref, off, tkp, lowp, st, nomask)

        st = lax.fori_loop(0, n_past, past_step, st)
    for c in range(len(chains)):
        acc = st[2 * c + 1]
        for p in range(PAIRS):
            base = 2 * tq * p
            o_ref[c * tq:(c + 1) * tq, p * LANES:(p + 1) * LANES] = jnp.where(
                lane_lo, acc[base:base + tq], acc[base + tq:base + 2 * tq])


def _stick_breaking(q, kt, vt, kl, nseq, past_kt, past_vt, pl_, C):
    n = q.shape[0]
    t = n // nseq
    tq = min(SB_TQ, t)
    tkd = min(SB_TK, t)
    nsub = max(1, tkd // tq)
    nseqs = max(1, SB_CHAINS // nsub)
    nq = t // (tq * nsub)
    qmap = lambda b, i: (b * nq + i, 0)
    cmap = lambda b, i: (kl, b, 0, 0)
    in_specs = [
        pl.BlockSpec((tq * nsub * nseqs, SB_DIM), qmap),
        pl.BlockSpec((None, nseqs, SB_DIM, t), cmap),
        pl.BlockSpec((None, nseqs, SB_DIM, t), cmap),
    ]
    args = [q, kt, vt]
    n_past = 0
    if past_kt is not None:
        plen = past_kt.shape[3]
        n_past = plen // SB_TK
        pmap = lambda b, i: (pl_, b, 0, 0)
        in_specs += [pl.BlockSpec((None, nseqs, SB_DIM, plen), pmap),
                     pl.BlockSpec((None, nseqs, SB_DIM, plen), pmap)]
        args += [past_kt, past_vt]
    in_specs.append(pl.BlockSpec((2 * tkd, tkd), lambda b, i: (0, 0)))
    args.append(C["low%d" % tkd])
    if n_past:
        in_specs.append(pl.BlockSpec((2 * SB_TK, SB_TK), lambda b, i: (0, 0)))
        args.append(C["low%d" % SB_TK])
    return pl.pallas_call(
        functools.partial(_sb_body, tq=tq, nsub=nsub, nseqs=nseqs, tkd=tkd, n_past=n_past, tkp=SB_TK),
        grid=(nseq // nseqs, nq),
        in_specs=in_specs,
        out_specs=pl.BlockSpec((tq * nsub * nseqs, SB_DIM), qmap),
        out_shape=jax.ShapeDtypeStruct((n, SB_DIM), F32),
        compiler_params=_params(("arbitrary", "arbitrary")),
        name="stick_breaking",
    )(*args)


def _xattn_body(x_ref, orw_ref, osb_ref, ocv_ref, wout_ref, g_ref, wq_ref, qg_ref, mk_ref, mv_ref, wo_ref, o_ref,
                *, nseqs, t):
    x = x_ref[...]
    x = x + _dot(orw_ref[...].astype(BF16), wout_ref[0:RW_DIM, :])
    x = x + _dot(osb_ref[...].astype(BF16), wout_ref[RW_DIM:RW_DIM + SB_DIM, :])
    x = x + _dot(ocv_ref[...].astype(BF16), wout_ref[RW_DIM + SB_DIM:D_MODEL, :])
    hx = _rms(x, g_ref[...]).astype(BF16)
    qf = _dot(hx, wq_ref[...])
    heads = []
    for hh in range(X_HEADS):
        cs = slice(hh * X_HEAD_DIM, (hh + 1) * X_HEAD_DIM)
        qh = _rms(qf[:, cs], qg_ref[...]).astype(BF16)
        outs = []
        for g in range(nseqs):
            s = _dot_nt(qh[g * t:(g + 1) * t], mk_ref[g, :, cs].astype(BF16)) * (X_HEAD_DIM ** -0.5)
            s = jnp.exp(s - jnp.max(s, axis=-1, keepdims=True))
            attn = s / jnp.sum(s, axis=-1, keepdims=True)
            outs.append(_dot(attn.astype(BF16), mv_ref[g, :, cs].astype(BF16)))
        heads.append(outs[0] if nseqs == 1 else jnp.concatenate(outs, axis=0))
    o = jnp.concatenate(heads, axis=1).astype(BF16)
    o_ref[...] = x + _dot(o, wo_ref[...])


def _xattn(x, orw, osb, ocv, l, nseq, P, mem_k, mem_v, mem_l):
    n = x.shape[0]
    rs = n // nseq
    t = min(XA_TQ, rs)
    nseqs = max(1, XA_TQ // rs)
    tq = t * nseqs
    nq = rs // t
    nm = mem_k.shape[2]
    row = lambda b, i: (b * nq + i, 0)
    mmap = lambda b, i: (mem_l, b, 0, 0)
    return pl.pallas_call(
        functools.partial(_xattn_body, nseqs=nseqs, t=t),
        grid=(nseq // nseqs, nq),
        in_specs=[
            pl.BlockSpec((tq, D_MODEL), row),
            pl.BlockSpec((tq, RW_DIM), row),
            pl.BlockSpec((tq, SB_DIM), row),
            pl.BlockSpec((tq, CONV_DIM), row),
            _lspec(l, (D_MODEL, D_MODEL)),
            _lspec(l, (1, D_MODEL)),
            _lspec(l, (D_MODEL, D_MODEL)),
            _lspec(l, (1, X_HEAD_DIM)),
            pl.BlockSpec((None, nseqs, nm, D_MODEL), mmap),
            pl.BlockSpec((None, nseqs, nm, D_MODEL), mmap),
            _lspec(l, (D_MODEL, D_MODEL)),
        ],
        out_specs=pl.BlockSpec((tq, D_MODEL), row),
        out_shape=jax.ShapeDtypeStruct((n, D_MODEL), F32),
        compiler_params=_params(("arbitrary", "arbitrary")),
        name="xattn",
    )(x, orw, osb, ocv, P["w_out"], P["x_norm"], P["x_wq"], P["x_q_norm"], mem_k, mem_v, P["x_wo"])


def _memkv_body(m_ref, g_ref, wk_ref, wv_ref, kg_ref, mk4_ref, mv4_ref, mk_ref, mv_ref):
    m = _rms(m_ref[...], g_ref[...]).astype(BF16)
    kf = _dot(m, wk_ref[...])
    vf = _dot(m, wv_ref[...])
    mv_ref[...] = vf
    for hh in range(X_HEADS):
        cs = slice(hh * X_HEAD_DIM, (hh + 1) * X_HEAD_DIM)
        kn = _rms(kf[:, cs], kg_ref[...])
        mk_ref[:, cs] = kn
        mk4_ref[:, hh, :] = kn
        mv4_ref[:, hh, :] = vf[:, cs]


def _memory_kv(mem, P):
    n = mem.shape[0]
    depth = P["x_wk"].shape[0]
    tm = min(MEM_TM, n)
    lay = lambda l, i: (l, 0, 0)
    out4 = pl.BlockSpec((None, tm, X_HEADS, X_HEAD_DIM), lambda l, i: (l, i, 0, 0))
    out2 = pl.BlockSpec((None, tm, D_MODEL), lambda l, i: (l, i, 0))
    shape4 = jax.ShapeDtypeStruct((depth, n, X_HEADS, X_HEAD_DIM), F32)
    shape2 = jax.ShapeDtypeStruct((depth, n, D_MODEL), F32)
    return pl.pallas_call(
        _memkv_body,
        grid=(depth, n // tm),
        in_specs=[
            pl.BlockSpec((tm, D_MODEL), lambda l, i: (i, 0)),
            pl.BlockSpec((None, 1, D_MODEL), lay),
            pl.BlockSpec((None, D_MODEL, D_MODEL), lay),
            pl.BlockSpec((None, D_MODEL, D_MODEL), lay),
            pl.BlockSpec((None, 1, X_HEAD_DIM), lay),
        ],
        out_specs=[out4, out4, out2, out2],
        out_shape=[shape4, shape4, shape2, shape2],
        compiler_params=_params(("arbitrary", "arbitrary")),
        name="memory_kv",
    )(mem, P["mem_norm"], P["x_wk"], P["x_wv"], P["x_k_norm"])


def _layer(x, l, nseq, mem_k, mem_v, mem_l, past_kt, past_vt, s0, shift0, rkvprev0, conv0, kt_all, vt_all, P, C):
    n = x.shape[0]
    rs = n // nseq
    x = _ffn(x, l, P["ffn1_norm"], P["ffn1_wg"], P["ffn1_wu"], P["ffn1_wd"])
    h, rkv, q, o_cv, conv_buf, kt_all, vt_all = _mix_proj(x, l, rs, P, C, conv0, kt_all, vt_all)
    o_rw, s_t = _rwkv(h, rkv, l, rs, shift0, rkvprev0, s0, P, C)
    if kt_all.shape[1] == nseq:
        kt_cur, vt_cur, kl = kt_all, vt_all, l
    else:
        def per_seq(a):
            return a[l, 0].reshape(SB_DIM, nseq, rs).transpose(1, 0, 2)[None]
        kt_cur, vt_cur, kl = per_seq(kt_all), per_seq(vt_all), 0
    o_sb = _stick_breaking(q, kt_cur, vt_cur, kl, nseq, past_kt, past_vt, l, C)
    x = _xattn(x, o_rw, o_sb, o_cv, l, nseq, P, mem_k, mem_v, mem_l)
    x = _ffn(x, l, P["ffn2_norm"], P["ffn2_wg"], P["ffn2_wu"], P["ffn2_wd"])
    h_last = h.reshape(nseq, rs, D_MODEL)[:, rs - 1]
    return x, s_t, h_last, conv_buf, kt_all, vt_all


def _chunk_sum_matrix(tm):
    i = lax.broadcasted_iota(jnp.int32, (tm, tm), 0)
    j = lax.broadcasted_iota(jnp.int32, (tm, tm), 1)
    same = (i // CHUNK) == (j // CHUNK)
    return jnp.concatenate([same & (j <= i), same], axis=0).astype(BF16)


def _lower_ones(n, strict):
    i = lax.broadcasted_iota(jnp.int32, (n, n), 0)
    j = lax.broadcasted_iota(jnp.int32, (n, n), 1)
    return ((j < i) if strict else (j <= i)).astype(BF16)


def kernel(x_prompt, x_sample, mem_prompt, cache_sb_k, cache_sb_v, state_rwkv, state_shift, state_conv,
           cache_mem_k, cache_mem_v, ffn1_norm, ffn1_wg, ffn1_wu, ffn1_wd, mix_norm, w_in, w_out, rw_mu_rkv,
           rw_mu_wag, rw_w0, rw_w1, rw_w2, rw_a0, rw_a1, rw_a2, rw_g1, rw_g2, rw_k_k, rw_k_a, rw_r_k, rw_ln_w,
           rw_ln_b, sb_q_norm, sb_k_norm, cv_w, x_norm, mem_norm, x_wq, x_wk, x_wv, x_wo, x_q_norm, x_k_norm,
           ffn2_norm, ffn2_wg, ffn2_wu, ffn2_wd):
    depth = w_in.shape[0]
    bp, tp, _ = x_prompt.shape
    bs, ts, _ = x_sample.shape
    n_mem = mem_prompt.shape[1]
    plen = cache_sb_k.shape[2]

    def vec(a):
        return a.reshape(a.shape[0], 1, -1)

    def per_head(a):
        return jnp.tile(a, (1, SB_HEADS))

    w_in_b = w_in.astype(BF16)
    P = dict(
        ffn1_norm=vec(ffn1_norm), ffn1_wg=ffn1_wg.astype(BF16), ffn1_wu=ffn1_wu.astype(BF16),
        ffn1_wd=ffn1_wd.astype(BF16),
        mix_norm=vec(mix_norm), w_rq=w_in_b[:, :, 0:_K0],
        w_kvt=jnp.swapaxes(w_in_b[:, :, _K0:_C0], 1, 2), w_cv=w_in_b[:, :, _C0:],
        w_out=w_out.astype(BF16),
        rw_mu_rkv=rw_mu_rkv.reshape(depth, 1, _RKV), rw_mu_wag=rw_mu_wag,
        rw_w0=vec(rw_w0), rw_w1=rw_w1.astype(BF16), rw_w2=rw_w2.astype(BF16),
        rw_a0=vec(rw_a0), rw_a1=rw_a1.astype(BF16), rw_a2=rw_a2.astype(BF16),
        rw_g1=rw_g1.astype(BF16), rw_g2=rw_g2.astype(BF16),
        rw_k_k=vec(rw_k_k), rw_k_a=vec(rw_k_a), rw_r_k=rw_r_k.reshape(depth, 1, RW_DIM),
        rw_ln_w=vec(rw_ln_w), rw_ln_b=vec(rw_ln_b),
        sb_q_norm=vec(per_head(sb_q_norm)), sb_k_norm_col=per_head(sb_k_norm).reshape(depth, SB_DIM, 1), cv_w=cv_w,
        x_norm=vec(x_norm), mem_norm=vec(mem_norm),
        x_wq=x_wq.astype(BF16), x_wk=x_wk.astype(BF16), x_wv=x_wv.astype(BF16), x_wo=x_wo.astype(BF16),
        x_q_norm=vec(x_q_norm), x_k_norm=vec(x_k_norm),
        ffn2_norm=vec(ffn2_norm), ffn2_wg=ffn2_wg.astype(BF16), ffn2_wu=ffn2_wu.astype(BF16),
        ffn2_wd=ffn2_wd.astype(BF16),
    )
    hid = lax.broadcasted_iota(jnp.int32, (LANES, LANES), 0) // HEAD_DIM
    hjd = lax.broadcasted_iota(jnp.int32, (LANES, LANES), 1) // HEAD_DIM
    C = {"e128": (hid == hjd).astype(BF16)}
    for tm in {min(RWKV_TM, bp * tp), min(RWKV_TM, bs * ts)}:
        C["cumtot%d" % tm] = _chunk_sum_matrix(tm)
    for size in {min(SB_TK, tp), min(SB_TK, ts), SB_TK}:
        C["low%d" % size] = jnp.tile(_lower_ones(size, True), (2, 1))

    dt = x_prompt.dtype
    zero_s = jnp.zeros((bp, RW_HEADS, HEAD_DIM, HEAD_DIM), state_rwkv.dtype)
    zero_shift = jnp.zeros((bp, 1, D_MODEL), dt)
    zero_rkv = jnp.zeros((bp, 1, _RKV), dt)
    zero_conv = jnp.zeros((bp, CONV_W - 1, CONV_DIM), dt)
    past_kt = jnp.transpose(cache_sb_k, (0, 1, 3, 4, 2)).reshape(depth, bs, SB_DIM, plen)
    past_vt = jnp.transpose(cache_sb_v, (0, 1, 3, 4, 2)).reshape(depth, bs, SB_DIM, plen)
    cmk = cache_mem_k.reshape(depth, bs, n_mem, D_MODEL)
    cmv = cache_mem_v.reshape(depth, bs, n_mem, D_MODEL)

    yp = x_prompt.reshape(bp * tp, D_MODEL)
    ys = x_sample.reshape(bs * ts, D_MODEL)
    mem = mem_prompt.reshape(bp * n_mem, D_MODEL)
    pkt = jnp.zeros((depth, bp, SB_DIM, tp), dt)
    pvt = jnp.zeros((depth, bp, SB_DIM, tp), dt)
    skt = jnp.zeros((depth, 1, SB_DIM, bs * ts), dt)
    svt = jnp.zeros((depth, 1, SB_DIM, bs * ts), dt)
    pmk, pmv, pmk2, pmv2 = _memory_kv(mem, P)
    pmk = pmk.reshape(depth, bp, n_mem, X_HEADS, X_HEAD_DIM)
    pmv = pmv.reshape(depth, bp, n_mem, X_HEADS, X_HEAD_DIM)
    pmk2 = pmk2.reshape(depth, bp, n_mem, D_MODEL)
    pmv2 = pmv2.reshape(depth, bp, n_mem, D_MODEL)
    pS, psh, pcv = [], [], []
    sS, ssh, scv = [], [], []
    for l in range(depth):
        yp, s_t, sh, cbuf, pkt, pvt = _layer(yp, l, bp, pmk2, pmv2, l, None, None, zero_s, zero_shift, zero_rkv,
                                             zero_conv, pkt, pvt, P, C)
        pS.append(s_t); psh.append(sh); pcv.append(cbuf)
        shift_l = state_shift[l]
        rkvprev = _state_proj(shift_l, l, P["w_rq"]).reshape(bs, 1, _RKV)
        ys, s_t, sh, cbuf, skt, svt = _layer(ys, l, bs, cmk, cmv, l, past_kt, past_vt, state_rwkv[l],
                                             shift_l.reshape(bs, 1, D_MODEL), rkvprev, state_conv[l], skt, svt, P, C)
        sS.append(s_t); ssh.append(sh); scv.append(cbuf)

    def stack(xs, shape):
        return jnp.stack(xs).reshape((depth,) + shape)

    def prompt_kv(a):
        return jnp.transpose(a.reshape(depth, bp, SB_HEADS, HEAD_DIM, tp), (0, 1, 4, 2, 3))

    def sample_kv(a):
        return jnp.transpose(a.reshape(depth, SB_HEADS, HEAD_DIM, bs, ts), (0, 3, 4, 1, 2))

    return (
        yp.reshape(bp, tp, D_MODEL), ys.reshape(bs, ts, D_MODEL),
        prompt_kv(pkt), prompt_kv(pvt),
        stack(pS, (bp, RW_HEADS, HEAD_DIM, HEAD_DIM)), stack(psh, (bp, D_MODEL)),
        stack(pcv, (bp, CONV_W - 1, CONV_DIM)),
        pmk, pmv,
        sample_kv(skt), sample_kv(svt),
        stack(sS, (bs, RW_HEADS, HEAD_DIM, HEAD_DIM)), stack(ssh, (bs, D_MODEL)),
        stack(scv, (bs, CONV_W - 1, CONV_DIM)),
    )
```

```python
import functools

import jax
import jax.numpy as jnp
from jax import lax
from jax.experimental import pallas as pl
from jax.experimental.pallas import tpu as pltpu

F32 = jnp.float32
BF16 = jnp.bfloat16

D_MODEL = 1024
HEAD_DIM = 64
RW_HEADS = 6
SB_HEADS = 6
RW_DIM = RW_HEADS * HEAD_DIM
SB_DIM = SB_HEADS * HEAD_DIM
CONV_DIM = 256
CONV_W = 3
FFN_DIM = 2816
X_HEADS = 4
X_HEAD_DIM = D_MODEL // X_HEADS
RMS_EPS = 1e-6
GN_EPS = 64e-5
L2_EPS = 1e-12
LOG2E = 1.4426950408889634

CHUNK = 64
LANES = 128
SUBLANES = 8
PAIRS = RW_DIM // LANES
VMEM_LIMIT = 56 * 1024 * 1024

FFN_TM = 1024
FFN_TF = 256
MIX_TM = 512
RWKV_TM = 256
SB_TQ = 128
SB_TK = 256
SB_CHAINS = 2
XA_TQ = 512
MEM_TM = 512

_RKV = 3 * RW_DIM
_Q0 = _RKV
_K0 = _Q0 + SB_DIM
_V0 = _K0 + SB_DIM
_C0 = _V0 + SB_DIM


def _dot(a, b):
    return jnp.dot(a, b, preferred_element_type=F32)


def _dot_nt(a, b):
    return lax.dot_general(a, b, (((1,), (1,)), ((), ())), preferred_element_type=F32)


def _dot_tn(a, b):
    return lax.dot_general(a, b, (((0,), (0,)), ((), ())), preferred_element_type=F32)


def _split(x, terms):
    parts = []
    rem = x
    for t in range(terms):
        p = rem.astype(BF16)
        parts.append(p)
        if t + 1 < terms:
            rem = rem - p.astype(F32)
    return parts


def _dot_exact_rhs(x, m, terms=3):
    acc = None
    for p in _split(x, terms):
        y = _dot(p, m)
        acc = y if acc is None else acc + y
    return acc


def _dot_exact_lhs(m, x, terms=3):
    acc = None
    for p in _split(x, terms):
        y = _dot(m, p)
        acc = y if acc is None else acc + y
    return acc


def _head_sums_lanes(x):
    lane_lo = _lane_lo()
    outs = []
    for p in range(x.shape[1] // LANES):
        xp = x[:, p * LANES:(p + 1) * LANES]
        lo = jnp.sum(jnp.where(lane_lo, xp, 0.0), axis=1, keepdims=True)
        hi = jnp.sum(jnp.where(lane_lo, 0.0, xp), axis=1, keepdims=True)
        outs.append(jnp.where(lane_lo, lo, hi))
    return jnp.concatenate(outs, axis=1)


def _rms(x, g):
    return x * lax.rsqrt(jnp.mean(x * x, axis=-1, keepdims=True) + RMS_EPS) * g


def _sigmoid(x):
    return 1.0 / (1.0 + jnp.exp(-x))


def _softplus(y):
    return jnp.maximum(y, 0.0) + jnp.log1p(jnp.exp(-jnp.abs(y)))


def _params(sem):
    return pltpu.CompilerParams(dimension_semantics=sem, vmem_limit_bytes=VMEM_LIMIT)


def _lspec(l, tail):
    nz = (0,) * len(tail)
    return pl.BlockSpec((None,) + tuple(tail), lambda *_: (l,) + nz)


def _lane_lo():
    return lax.broadcasted_iota(jnp.int32, (1, LANES), 1) < HEAD_DIM


def _pair_stack(xp, lane_lo):
    return jnp.concatenate([jnp.where(lane_lo, xp, 0.0), jnp.where(lane_lo, 0.0, xp)], axis=0)


def _ffn_body(x_ref, g_ref, wg_ref, wu_ref, wd_ref, o_ref):
    x = x_ref[...]
    h = _rms(x, g_ref[...]).astype(BF16)
    acc = None
    for f in range(FFN_DIM // FFN_TF):
        fs = slice(f * FFN_TF, (f + 1) * FFN_TF)
        gate = _dot(h, wg_ref[:, fs])
        up = _dot(h, wu_ref[:, fs])
        act = (gate * _sigmoid(gate) * up).astype(BF16)
        part = _dot(act, wd_ref[fs, :])
        acc = part if acc is None else acc + part
    o_ref[...] = x + 0.5 * acc


def _ffn(x, l, norm, wg, wu, wd):
    n = x.shape[0]
    tm = min(FFN_TM, n)
    once = pl.Buffered(1)
    return pl.pallas_call(
        _ffn_body,
        grid=(n // tm,),
        in_specs=[
            pl.BlockSpec((tm, D_MODEL), lambda i: (i, 0)),
            _lspec(l, (1, D_MODEL)),
            pl.BlockSpec((None, D_MODEL, FFN_DIM), lambda i: (l, 0, 0), pipeline_mode=once),
            pl.BlockSpec((None, D_MODEL, FFN_DIM), lambda i: (l, 0, 0), pipeline_mode=once),
            pl.BlockSpec((None, FFN_DIM, D_MODEL), lambda i: (l, 0, 0), pipeline_mode=once),
        ],
        out_specs=pl.BlockSpec((tm, D_MODEL), lambda i: (i, 0)),
        out_shape=jax.ShapeDtypeStruct((n, D_MODEL), F32),
        compiler_params=_params(("arbitrary",)),
        name="ffn",
    )(x, norm, wg, wu, wd)


def _mix_body(x_ref, g_ref, wrq_ref, wkv_ref, wcv_ref, qg_ref, kg_ref, cw_ref, c0_ref, kin_ref, vin_ref,
              h_ref, rkv_ref, q_ref, ocv_ref, cbuf_ref, kt_ref, vt_ref, tail_scr, *, tm, rs):
    del kin_ref, vin_ref
    i = pl.program_id(0)
    h = _rms(x_ref[...], g_ref[...])
    h_ref[...] = h
    hb = h.astype(BF16)
    rkv_ref[...] = _dot(hb, wrq_ref[:, 0:_RKV])
    sq = _dot(hb, wrq_ref[:, _RKV:_RKV + SB_DIM])
    ms = _head_sums_lanes(sq * sq) * (1.0 / HEAD_DIM)
    q_ref[...] = sq * lax.rsqrt(ms + RMS_EPS) * qg_ref[...] * (HEAD_DIM ** -0.5 * LOG2E)

    kt = _dot_nt(wkv_ref[0:SB_DIM, :], hb)
    k3 = kt.reshape(SB_HEADS, HEAD_DIM, tm)
    kms = jnp.mean(k3 * k3, axis=1, keepdims=True)
    kt_ref[...] = (k3 * lax.rsqrt(kms + RMS_EPS)).reshape(SB_DIM, tm) * kg_ref[...]
    vt_ref[...] = _dot_nt(wkv_ref[SB_DIM:2 * SB_DIM, :], hb)

    cb = _dot(hb, wcv_ref[:, 0:CONV_DIM])
    u = _dot(hb, wcv_ref[:, CONV_DIM:2 * CONV_DIM]) * _dot(hb, wcv_ref[:, 2 * CONV_DIM:3 * CONV_DIM])
    w0 = cw_ref[0:1, :]
    w1 = cw_ref[1:2, :]
    w2 = cw_ref[2:3, :]
    seg = min(tm, rs)
    nseg = tm // seg
    if rs >= tm:
        tiles_per_seq = rs // tm

        @pl.when(i % tiles_per_seq == 0)
        def _():
            tail_scr[SUBLANES - 2:SUBLANES, :] = c0_ref[0]

    row = lax.broadcasted_iota(jnp.int32, (seg, CONV_DIM), 0)
    for s in range(nseg):
        us = u[s * seg:(s + 1) * seg]
        if rs >= tm:
            b0 = tail_scr[SUBLANES - 2:SUBLANES - 1, :]
            b1 = tail_scr[SUBLANES - 1:SUBLANES, :]
        else:
            b0 = c0_ref[s, 0:1, :]
            b1 = c0_ref[s, 1:2, :]
        p1 = jnp.where(row == 0, b1, pltpu.roll(us, 1, 0))
        p2 = jnp.where(row == 0, b0, jnp.where(row == 1, b1, pltpu.roll(us, 2, 0)))
        y = p2 * w0 + p1 * w1 + us * w2
        ocv_ref[s * seg:(s + 1) * seg, :] = cb[s * seg:(s + 1) * seg] * y
        tail_scr[...] = us[seg - SUBLANES:seg]
        cbuf_ref[s] = tail_scr[SUBLANES - 2:SUBLANES, :]


def _mix_proj(x, l, rs, P, C, conv0, kt_all, vt_all):
    n = x.shape[0]
    tm = min(MIX_TM, n)
    nseq = n // rs
    if rs >= tm:
        tps = rs // tm
        nsb = 1
        smap = lambda i: (i // tps, 0, 0)
        tmap = lambda i: (l, i // tps, 0, i % tps)
    else:
        nsb = tm // rs
        smap = lambda i: (i, 0, 0)
        tmap = lambda i: (l, 0, 0, i)
    row = lambda i: (i, 0)
    tspec = pl.BlockSpec((None, None, SB_DIM, tm), tmap)
    return pl.pallas_call(
        functools.partial(_mix_body, tm=tm, rs=rs),
        grid=(n // tm,),
        in_specs=[
            pl.BlockSpec((tm, D_MODEL), row),
            _lspec(l, (1, D_MODEL)),
            _lspec(l, (D_MODEL, _RKV + SB_DIM)),
            _lspec(l, (2 * SB_DIM, D_MODEL)),
            _lspec(l, (D_MODEL, 3 * CONV_DIM)),
            _lspec(l, (1, SB_DIM)),
            _lspec(l, (SB_DIM, 1)),
            _lspec(l, (CONV_W, CONV_DIM)),
            pl.BlockSpec((nsb, CONV_W - 1, CONV_DIM), smap),
            pl.BlockSpec(memory_space=pl.ANY),
            pl.BlockSpec(memory_space=pl.ANY),
        ],
        out_specs=[
            pl.BlockSpec((tm, D_MODEL), row),
            pl.BlockSpec((tm, _RKV), row),
            pl.BlockSpec((tm, SB_DIM), row),
            pl.BlockSpec((tm, CONV_DIM), row),
            pl.BlockSpec((nsb, CONV_W - 1, CONV_DIM), smap),
            tspec,
            tspec,
        ],
        out_shape=[
            jax.ShapeDtypeStruct((n, D_MODEL), F32),
            jax.ShapeDtypeStruct((n, _RKV), F32),
            jax.ShapeDtypeStruct((n, SB_DIM), F32),
            jax.ShapeDtypeStruct((n, CONV_DIM), F32),
            jax.ShapeDtypeStruct((nseq, CONV_W - 1, CONV_DIM), F32),
            jax.ShapeDtypeStruct(kt_all.shape, F32),
            jax.ShapeDtypeStruct(vt_all.shape, F32),
        ],
        input_output_aliases={9: 5, 10: 6},
        scratch_shapes=[pltpu.VMEM((SUBLANES, CONV_DIM), F32)],
        compiler_params=_params(("arbitrary",)),
        name="mix_proj",
    )(x, P["mix_norm"], P["w_rq"], P["w_kvt"], P["w_cv"], P["sb_q_norm"], P["sb_k_norm_col"],
      P["cv_w"], conv0, kt_all, vt_all)


def _state_proj_body(s_ref, w_ref, o_ref):
    o_ref[...] = _dot(s_ref[...].astype(BF16), w_ref[:, 0:_RKV])


def _state_proj(shift0, l, w_rq):
    b = shift0.shape[0]
    return pl.pallas_call(
        _state_proj_body,
        grid=(1,),
        in_specs=[pl.BlockSpec((b, D_MODEL), lambda i: (0, 0)), _lspec(l, (D_MODEL, _RKV + SB_DIM))],
        out_specs=pl.BlockSpec((b, _RKV), lambda i: (0, 0)),
        out_shape=jax.ShapeDtypeStruct((b, _RKV), F32),
        compiler_params=_params(("arbitrary",)),
        name="state_proj",
    )(shift0, w_rq)


def _unit_lower_inverses(lmats):
    n = lmats[0].shape[0]
    eye = (lax.broadcasted_iota(jnp.int32, (n, n), 0) == lax.broadcasted_iota(jnp.int32, (n, n), 1)).astype(F32)
    ps = [eye + lm for lm in lmats]
    lks = []
    for lm in lmats:
        lb = lm.astype(BF16)
        lks.append(_dot(lb, lb))
    power = 2
    while power < CHUNK:
        for t in range(len(lmats)):
            lkb = lks[t].astype(BF16)
            if 2 * power < CHUNK:
                both = _dot(lkb, jnp.concatenate([lks[t], ps[t]], axis=1).astype(BF16))
                lks[t] = both[:, 0:n]
                ps[t] = ps[t] + both[:, n:2 * n]
            else:
                ps[t] = ps[t] + _dot(lkb, ps[t].astype(BF16))
        power *= 2
    return ps


def _state_to_pairs(s_ref, sq):
    zero = jnp.zeros((HEAD_DIM, HEAD_DIM), F32)
    out = []
    for p in range(PAIRS):
        top = jnp.concatenate([s_ref[sq, 2 * p], zero], axis=1)
        bot = jnp.concatenate([zero, s_ref[sq, 2 * p + 1]], axis=1)
        out.append(jnp.concatenate([top, bot], axis=0))
    return out


def _pairs_to_state(sbd, sout_ref, sq):
    for p in range(PAIRS):
        sout_ref[sq, 2 * p] = sbd[p][0:HEAD_DIM, 0:HEAD_DIM]
        sout_ref[sq, 2 * p + 1] = sbd[p][HEAD_DIM:LANES, HEAD_DIM:LANES]


def _rwkv_body(h_ref, rkv_ref, sh0_ref, rp0_ref, s0_ref,
               murkv_ref, muwag_ref, w0_ref, w1_ref, w2_ref, a0_ref, a1_ref, a2_ref, g1_ref, g2_ref,
               kk_ref, ka_ref, rk_ref, lnw_ref, lnb_ref, csum_ref,
               o_ref, sout_ref, hl_scr, rl_scr, s_scr, *, tm, rs):
    i = pl.program_id(0)
    seg = min(tm, rs)
    nseg = tm // seg
    nchunk = tm // CHUNK
    carried = rs >= tm
    if carried:
        tiles_per_seq = rs // tm

        @pl.when(i % tiles_per_seq == 0)
        def _():
            hl_scr[SUBLANES - 1:SUBLANES, :] = sh0_ref[0]
            rl_scr[SUBLANES - 1:SUBLANES, :] = rp0_ref[0]
            init = _state_to_pairs(s0_ref, 0)
            for p in range(PAIRS):
                s_scr[p] = init[p]

    h = h_ref[...]
    rkv = rkv_ref[...]
    if carried:
        hprev_rows = hl_scr[SUBLANES - 1:SUBLANES, :]
        rprev_rows = rl_scr[SUBLANES - 1:SUBLANES, :]
    else:
        hprev_rows = jnp.concatenate(
            [jnp.broadcast_to(sh0_ref[s], (seg, D_MODEL)) for s in range(nseg)], axis=0)
        rprev_rows = jnp.concatenate(
            [jnp.broadcast_to(rp0_ref[s], (seg, _RKV)) for s in range(nseg)], axis=0)
    first_h = (lax.broadcasted_iota(jnp.int32, (tm, D_MODEL), 0) & (seg - 1)) == 0
    first_r = (lax.broadcasted_iota(jnp.int32, (tm, _RKV), 0) & (seg - 1)) == 0
    hp = jnp.where(first_h, hprev_rows, pltpu.roll(h, 1, 0))
    rp = jnp.where(first_r, rprev_rows, pltpu.roll(rkv, 1, 0))
    if carried:
        hl_scr[...] = h[tm - SUBLANES:tm]
        rl_scr[...] = rkv[tm - SUBLANES:tm]

    rkv = rkv + murkv_ref[...] * (rp - rkv)
    r = rkv[:, 0:RW_DIM]
    k = rkv[:, RW_DIM:2 * RW_DIM]
    v = rkv[:, 2 * RW_DIM:3 * RW_DIM]
    dx = hp - h
    xw = (h + muwag_ref[0:1, :] * dx).astype(BF16)
    xa = (h + muwag_ref[1:2, :] * dx).astype(BF16)
    xg = (h + muwag_ref[2:3, :] * dx).astype(BF16)
    wl = w0_ref[...] + _dot(jnp.tanh(_dot(xw, w1_ref[...])).astype(BF16), w2_ref[...])
    w = -_softplus(-wl) - 0.5
    logdecay = -jnp.exp(w)
    a = _sigmoid(a0_ref[...] + _dot(_dot(xa, a1_ref[...]).astype(BF16), a2_ref[...]))
    gate = _dot(_sigmoid(_dot(xg, g1_ref[...])).astype(BF16), g2_ref[...])
    kk = k * kk_ref[...]
    kk = kk * lax.rsqrt(_head_sums_lanes(kk * kk) + L2_EPS)
    k = k * (1.0 + (a - 1.0) * ka_ref[...])
    b = kk * a
    bonus = _head_sums_lanes(r * k * rk_ref[...]) * v

    ci = lax.broadcasted_iota(jnp.int32, (LANES, LANES), 0)
    cj = lax.broadcasted_iota(jnp.int32, (LANES, LANES), 1)
    strict = cj < ci
    incl = cj <= ci
    lane_lo = _lane_lo()

    sums = _dot_exact_lhs(csum_ref[...], logdecay, terms=2)
    cum = sums[0:tm]
    tot = sums[tm:2 * tm]
    g_incl = jnp.exp(cum)
    g_prev = jnp.exp(cum - logdecay)
    g_inv = jnp.exp(-cum)
    g_rem = jnp.exp(tot - cum)
    g_tot = jnp.exp(tot)
    rq = r * g_incl
    kq = k * g_inv
    bq = b * g_inv
    aq = -kk * g_prev
    kz = k * g_rem
    bz = b * g_rem
    items = [(c, p) for c in range(nchunk) for p in range(PAIRS)]

    def stacked(x, c, p):
        return _pair_stack(x[c * CHUNK:(c + 1) * CHUNK, p * LANES:(p + 1) * LANES], lane_lo).astype(BF16)

    work = [dict(aq=stacked(aq, c, p), rq=stacked(rq, c, p), v=stacked(v, c, p),
                 kzbz=jnp.concatenate([stacked(kz, c, p), stacked(bz, c, p)], axis=0),
                 gtot=g_tot[c * CHUNK:c * CHUNK + 1, p * LANES:(p + 1) * LANES]) for c, p in items]
    quads = [_dot_nt(jnp.concatenate([wk["aq"], wk["rq"]], axis=0),
                     jnp.concatenate([stacked(kq, c, p), stacked(bq, c, p)], axis=0))
             for wk, (c, p) in zip(work, items)]
    lmats = []
    for wk, quad in zip(work, quads):
        wk["m_ak"] = jnp.where(strict, quad[0:LANES, 0:LANES], 0.0).astype(BF16)
        lmats.append(jnp.where(strict, quad[0:LANES, LANES:2 * LANES], 0.0))
        wk["mr"] = jnp.concatenate([jnp.where(incl, quad[LANES:2 * LANES, 0:LANES], 0.0),
                                    jnp.where(incl, quad[LANES:2 * LANES, LANES:2 * LANES], 0.0)],
                                   axis=1).astype(BF16)
    for wk in work:
        wk["x1"] = _dot(wk["m_ak"], wk["v"])
    tinvs = _unit_lower_inverses(lmats)
    for wk, tinv in zip(work, tinvs):
        wu = _dot(tinv.astype(BF16), jnp.concatenate([wk["aq"], wk["x1"].astype(BF16)], axis=1))
        wk["wr"] = jnp.concatenate([wu[:, 0:LANES].astype(BF16), wk["rq"]], axis=0)
        wk["u0"] = wu[:, LANES:2 * LANES]

    if carried:
        sbd = [s_scr[p] for p in range(PAIRS)]
    o_chunks = []
    for c in range(nchunk):
        sq = (c * CHUNK) // seg
        if not carried:
            sbd = _state_to_pairs(s0_ref, sq)
        wks = work[c * PAIRS:(c + 1) * PAIRS]
        ys = [_dot_nt(wks[p]["wr"], sbd[p].astype(BF16)) for p in range(PAIRS)]
        vus = [jnp.concatenate([wks[p]["v"], (ys[p][0:LANES] + wks[p]["u0"]).astype(BF16)], axis=0)
               for p in range(PAIRS)]
        sbd = [sbd[p] * wks[p]["gtot"] + _dot_tn(vus[p], wks[p]["kzbz"]) for p in range(PAIRS)]
        o_ss = [ys[p][LANES:2 * LANES] + _dot(wks[p]["mr"], vus[p]) for p in range(PAIRS)]
        if not carried:
            _pairs_to_state(sbd, sout_ref, sq)
        o_chunks.append(jnp.concatenate([o_s[0:CHUNK] + o_s[CHUNK:2 * CHUNK] for o_s in o_ss], axis=1))
    if carried:
        for p in range(PAIRS):
            s_scr[p] = sbd[p]
        _pairs_to_state(sbd, sout_ref, 0)
    o = jnp.concatenate(o_chunks, axis=0)
    mean = _head_sums_lanes(o) * (1.0 / HEAD_DIM)
    d = o - mean
    var = _head_sums_lanes(d * d) * (1.0 / HEAD_DIM)
    on = d * lax.rsqrt(var + GN_EPS) * lnw_ref[...] + lnb_ref[...]
    o_ref[...] = (on + bonus) * gate


def _rwkv(h, rkv, l, rs, shift0, rkvprev0, s0, P, C):
    n = h.shape[0]
    tm = min(RWKV_TM, n)
    nseq = n // rs
    if rs >= tm:
        tps = rs // tm
        nsb = 1
        smap3 = lambda i: (i // tps, 0, 0)
        smap4 = lambda i: (i // tps, 0, 0, 0)
    else:
        nsb = tm // rs
        smap3 = lambda i: (i, 0, 0)
        smap4 = lambda i: (i, 0, 0, 0)
    row = lambda i: (i, 0)
    lora_w, lora_a, lora_g = P["rw_w1"].shape[-1], P["rw_a1"].shape[-1], P["rw_g1"].shape[-1]
    return pl.pallas_call(
        functools.partial(_rwkv_body, tm=tm, rs=rs),
        grid=(n // tm,),
        in_specs=[
            pl.BlockSpec((tm, D_MODEL), row),
            pl.BlockSpec((tm, _RKV), row),
            pl.BlockSpec((nsb, 1, D_MODEL), smap3),
            pl.BlockSpec((nsb, 1, _RKV), smap3),
            pl.BlockSpec((nsb, RW_HEADS, HEAD_DIM, HEAD_DIM), smap4),
            _lspec(l, (1, _RKV)),
            _lspec(l, (3, D_MODEL)),
            _lspec(l, (1, RW_DIM)),
            _lspec(l, (D_MODEL, lora_w)),
            _lspec(l, (lora_w, RW_DIM)),
            _lspec(l, (1, RW_DIM)),
            _lspec(l, (D_MODEL, lora_a)),
            _lspec(l, (lora_a, RW_DIM)),
            _lspec(l, (D_MODEL, lora_g)),
            _lspec(l, (lora_g, RW_DIM)),
            _lspec(l, (1, RW_DIM)),
            _lspec(l, (1, RW_DIM)),
            _lspec(l, (1, RW_DIM)),
            _lspec(l, (1, RW_DIM)),
            _lspec(l, (1, RW_DIM)),
            pl.BlockSpec((2 * tm, tm), lambda i: (0, 0)),
        ],
        out_specs=[
            pl.BlockSpec((tm, RW_DIM), row),
            pl.BlockSpec((nsb, RW_HEADS, HEAD_DIM, HEAD_DIM), smap4),
        ],
        out_shape=[
            jax.ShapeDtypeStruct((n, RW_DIM), F32),
            jax.ShapeDtypeStruct((nseq, RW_HEADS, HEAD_DIM, HEAD_DIM), F32),
        ],
        scratch_shapes=[
            pltpu.VMEM((SUBLANES, D_MODEL), F32),
            pltpu.VMEM((SUBLANES, _RKV), F32),
            pltpu.VMEM((PAIRS, LANES, LANES), F32),
        ],
        compiler_params=_params(("arbitrary",)),
        name="rwkv",
    )(h, rkv, shift0, rkvprev0, s0,
      P["rw_mu_rkv"], P["rw_mu_wag"], P["rw_w0"], P["rw_w1"], P["rw_w2"], P["rw_a0"], P["rw_a1"],
      P["rw_a2"], P["rw_g1"], P["rw_g2"], P["rw_k_k"], P["rw_k_a"], P["rw_r_k"], P["rw_ln_w"],
      P["rw_ln_b"], C["cumtot%d" % tm])


def _sb_logs(z, mask):
    sp = jnp.log2(1.0 + jnp.exp2(-jnp.abs(z)))
    log_beta = jnp.minimum(z, 0.0) - sp
    log_rest = log_beta - z
    if mask is not None:
        log_rest = jnp.where(mask, log_rest, 0.0)
    return log_beta, log_rest


def _sb_weights(log_beta, log_rest, tail, carry, mask):
    wts = jnp.exp2(log_beta + tail + carry)
    if mask is not None:
        wts = jnp.where(mask, wts, 0.0)
    return wts.astype(BF16), carry + tail[:, 0:1] + log_rest[:, 0:1]


def _sb_body(*refs, tq, nsub, nseqs, tkd, n_past, tkp):
    if n_past:
        q_ref, kc_ref, vc_ref, kp_ref, vp_ref, lowd_ref, lowp_ref, o_ref = refs
    else:
        q_ref, kc_ref, vc_ref, lowd_ref, o_ref = refs
    qi = pl.program_id(1)
    lane_lo = _lane_lo()
    rows = SB_HEADS * tq
    chains = [(g, s) for g in range(nseqs) for s in range(nsub)]
    qs = [[_pair_stack(q_ref[c * tq:(c + 1) * tq, p * LANES:(p + 1) * LANES], lane_lo).astype(BF16)
           for p in range(PAIRS)] for c in range(len(chains))]

    def block(ref, g, p, off, tk):
        cols = slice(None) if ref.shape[2] == tk else pl.ds(off, tk)
        return ref[g, p * LANES:(p + 1) * LANES, cols].astype(BF16)

    def step(k_ref, v_ref, off, tk, lower, st, masks):
        kbs = [[block(k_ref, g, p, off, tk) for p in range(PAIRS)] for g in range(nseqs)]
        vbs = [[block(v_ref, g, p, off, tk) for p in range(PAIRS)] for g in range(nseqs)]
        zs = [jnp.concatenate([_dot(qs[c][p], kbs[g][p]) for p in range(PAIRS)], axis=0)
              for c, (g, _) in enumerate(chains)]
        logs, tails = [], []
        for c in range(len(chains)):
            logs.append(_sb_logs(zs[c], masks[c]))
            hi, lo = _split(logs[c][1], 2)
            tails.append(_dot(jnp.concatenate([hi, lo], axis=1), lower))
        out = []
        for c, (g, _) in enumerate(chains):
            wts, carry = _sb_weights(logs[c][0], logs[c][1], tails[c], st[2 * c], masks[c])
            pv = jnp.concatenate(
                [_dot_nt(wts[2 * tq * p:2 * tq * (p + 1)], vbs[g][p]) for p in range(PAIRS)], axis=0)
            out += [carry, st[2 * c + 1] + pv]
        return tuple(out)

    lowd = lowd_ref[...]
    q0 = qi * (tq * nsub)
    jd = q0 // tkd
    offd = pl.multiple_of(jd * tkd, tkd)
    kpos = offd + lax.broadcasted_iota(jnp.int32, (rows, tkd), 1)
    rowq = lax.broadcasted_iota(jnp.int32, (rows, tkd), 0) & (tq - 1)
    masks = [kpos < (q0 + s * tq + rowq) for _, s in chains]
    st = (jnp.zeros((rows, 1), F32), jnp.zeros((rows, LANES), F32)) * len(chains)
    st = step(kc_ref, vc_ref, offd, tkd, lowd, st, masks)
    nomask = [None] * len(chains)

    def cur_step(jj, st):
        off = pl.multiple_of((jd - 1 - jj) * tkd, tkd)
        return step(kc_ref, vc_ref, off, tkd, lowd, st, nomask)

    st = lax.fori_loop(0, jd, cur_step, st)
    if n_past:
        lowp = lowp_ref[...]

        def past_step(jj, st):
            off = pl.multiple_of((n_past - 1 - jj) * tkp, tkp)
            return step(kp_ref, vp_ref, off, tkp, lowp, st, nomask)

        st = lax.fori_loop(0, n_past, past_step, st)
    for c in range(len(chains)):
        acc = st[2 * c + 1]
        for p in range(PAIRS):
            base = 2 * tq * p
            o_ref[c * tq:(c + 1) * tq, p * LANES:(p + 1) * LANES] = jnp.where(
                lane_lo, acc[base:base + tq], acc[base + tq:base + 2 * tq])


def _stick_breaking(q, kt, vt, kl, nseq, past_kt, past_vt, pl_, C):
    n = q.shape[0]
    t = n // nseq
    tq = min(SB_TQ, t)
    tkd = min(SB_TK, t)
    nsub = max(1, tkd // tq)
    nseqs = max(1, SB_CHAINS // nsub)
    nq = t // (tq * nsub)
    qmap = lambda b, i: (b * nq + i, 0)
    cmap = lambda b, i: (kl, b, 0, 0)
    in_specs = [
        pl.BlockSpec((tq * nsub * nseqs, SB_DIM), qmap),
        pl.BlockSpec((None, nseqs, SB_DIM, t), cmap),
        pl.BlockSpec((None, nseqs, SB_DIM, t), cmap),
    ]
    args = [q, kt, vt]
    n_past = 0
    if past_kt is not None:
        plen = past_kt.shape[3]
        n_past = plen // SB_TK
        pmap = lambda b, i: (pl_, b, 0, 0)
        in_specs += [pl.BlockSpec((None, nseqs, SB_DIM, plen), pmap),
                     pl.BlockSpec((None, nseqs, SB_DIM, plen), pmap)]
        args += [past_kt, past_vt]
    in_specs.append(pl.BlockSpec((2 * tkd, tkd), lambda b, i: (0, 0)))
    args.append(C["low%d" % tkd])
    if n_past:
        in_specs.append(pl.BlockSpec((2 * SB_TK, SB_TK), lambda b, i: (0, 0)))
        args.append(C["low%d" % SB_TK])
    return pl.pallas_call(
        functools.partial(_sb_body, tq=tq, nsub=nsub, nseqs=nseqs, tkd=tkd, n_past=n_past, tkp=SB_TK),
        grid=(nseq // nseqs, nq),
        in_specs=in_specs,
        out_specs=pl.BlockSpec((tq * nsub * nseqs, SB_DIM), qmap),
        out_shape=jax.ShapeDtypeStruct((n, SB_DIM), F32),
        compiler_params=_params(("arbitrary", "arbitrary")),
        name="stick_breaking",
    )(*args)


def _xattn_body(x_ref, orw_ref, osb_ref, ocv_ref, wout_ref, g_ref, wq_ref, qg_ref, mk_ref, mv_ref, wo_ref, o_ref,
                *, nseqs, t):
    x = x_ref[...]
    x = x + _dot(orw_ref[...].astype(BF16), wout_ref[0:RW_DIM, :])
    x = x + _dot(osb_ref[...].astype(BF16), wout_ref[RW_DIM:RW_DIM + SB_DIM, :])
    x = x + _dot(ocv_ref[...].astype(BF16), wout_ref[RW_DIM + SB_DIM:D_MODEL, :])
    hx = _rms(x, g_ref[...]).astype(BF16)
    qf = _dot(hx, wq_ref[...])
    heads = []
    for hh in range(X_HEADS):
        cs = slice(hh * X_HEAD_DIM, (hh + 1) * X_HEAD_DIM)
        qh = _rms(qf[:, cs], qg_ref[...]).astype(BF16)
        outs = []
        for g in range(nseqs):
            s = _dot_nt(qh[g * t:(g + 1) * t], mk_ref[g, :, cs].astype(BF16)) * (X_HEAD_DIM ** -0.5)
            s = jnp.exp(s - jnp.max(s, axis=-1, keepdims=True))
            attn = s / jnp.sum(s, axis=-1, keepdims=True)
            outs.append(_dot(attn.astype(BF16), mv_ref[g, :, cs].astype(BF16)))
        heads.append(outs[0] if nseqs == 1 else jnp.concatenate(outs, axis=0))
    o = jnp.concatenate(heads, axis=1).astype(BF16)
    o_ref[...] = x + _dot(o, wo_ref[...])


def _xattn(x, orw, osb, ocv, l, nseq, P, mem_k, mem_v, mem_l):
    n = x.shape[0]
    rs = n // nseq
    t = min(XA_TQ, rs)
    nseqs = max(1, XA_TQ // rs)
    tq = t * nseqs
    nq = rs // t
    nm = mem_k.shape[2]
    row = lambda b, i: (b * nq + i, 0)
    mmap = lambda b, i: (mem_l, b, 0, 0)
    return pl.pallas_call(
        functools.partial(_xattn_body, nseqs=nseqs, t=t),
        grid=(nseq // nseqs, nq),
        in_specs=[
            pl.BlockSpec((tq, D_MODEL), row),
            pl.BlockSpec((tq, RW_DIM), row),
            pl.BlockSpec((tq, SB_DIM), row),
            pl.BlockSpec((tq, CONV_DIM), row),
            _lspec(l, (D_MODEL, D_MODEL)),
            _lspec(l, (1, D_MODEL)),
            _lspec(l, (D_MODEL, D_MODEL)),
            _lspec(l, (1, X_HEAD_DIM)),
            pl.BlockSpec((None, nseqs, nm, D_MODEL), mmap),
            pl.BlockSpec((None, nseqs, nm, D_MODEL), mmap),
            _lspec(l, (D_MODEL, D_MODEL)),
        ],
        out_specs=pl.BlockSpec((tq, D_MODEL), row),
        out_shape=jax.ShapeDtypeStruct((n, D_MODEL), F32),
        compiler_params=_params(("arbitrary", "arbitrary")),
        name="xattn",
    )(x, orw, osb, ocv, P["w_out"], P["x_norm"], P["x_wq"], P["x_q_norm"], mem_k, mem_v, P["x_wo"])


def _memkv_body(m_ref, g_ref, wk_ref, wv_ref, kg_ref, mk4_ref, mv4_ref, mk_ref, mv_ref):
    m = _rms(m_ref[...], g_ref[...]).astype(BF16)
    kf = _dot(m, wk_ref[...])
    vf = _dot(m, wv_ref[...])
    mv_ref[...] = vf
    for hh in range(X_HEADS):
        cs = slice(hh * X_HEAD_DIM, (hh + 1) * X_HEAD_DIM)
        kn = _rms(kf[:, cs], kg_ref[...])
        mk_ref[:, cs] = kn
        mk4_ref[:, hh, :] = kn
        mv4_ref[:, hh, :] = vf[:, cs]


def _memory_kv(mem, P):
    n = mem.shape[0]
    depth = P["x_wk"].shape[0]
    tm = min(MEM_TM, n)
    lay = lambda l, i: (l, 0, 0)
    out4 = pl.BlockSpec((None, tm, X_HEADS, X_HEAD_DIM), lambda l, i: (l, i, 0, 0))
    out2 = pl.BlockSpec((None, tm, D_MODEL), lambda l, i: (l, i, 0))
    shape4 = jax.ShapeDtypeStruct((depth, n, X_HEADS, X_HEAD_DIM), F32)
    shape2 = jax.ShapeDtypeStruct((depth, n, D_MODEL), F32)
    return pl.pallas_call(
        _memkv_body,
        grid=(depth, n // tm),
        in_specs=[
            pl.BlockSpec((tm, D_MODEL), lambda l, i: (i, 0)),
            pl.BlockSpec((None, 1, D_MODEL), lay),
            pl.BlockSpec((None, D_MODEL, D_MODEL), lay),
            pl.BlockSpec((None, D_MODEL, D_MODEL), lay),
            pl.BlockSpec((None, 1, X_HEAD_DIM), lay),
        ],
        out_specs=[out4, out4, out2, out2],
        out_shape=[shape4, shape4, shape2, shape2],
        compiler_params=_params(("arbitrary", "arbitrary")),
        name="memory_kv",
    )(mem, P["mem_norm"], P["x_wk"], P["x_wv"], P["x_k_norm"])


def _layer(x, l, nseq, mem_k, mem_v, mem_l, past_kt, past_vt, s0, shift0, rkvprev0, conv0, kt_all, vt_all, P, C):
    n = x.shape[0]
    rs = n // nseq
    x = _ffn(x, l, P["ffn1_norm"], P["ffn1_wg"], P["ffn1_wu"], P["ffn1_wd"])
    h, rkv, q, o_cv, conv_buf, kt_all, vt_all = _mix_proj(x, l, rs, P, C, conv0, kt_all, vt_all)
    o_rw, s_t = _rwkv(h, rkv, l, rs, shift0, rkvprev0, s0, P, C)
    if kt_all.shape[1] == nseq:
        kt_cur, vt_cur, kl = kt_all, vt_all, l
    else:
        def per_seq(a):
            return a[l, 0].reshape(SB_DIM, nseq, rs).transpose(1, 0, 2)[None]
        kt_cur, vt_cur, kl = per_seq(kt_all), per_seq(vt_all), 0
    o_sb = _stick_breaking(q, kt_cur, vt_cur, kl, nseq, past_kt, past_vt, l, C)
    x = _xattn(x, o_rw, o_sb, o_cv, l, nseq, P, mem_k, mem_v, mem_l)
    x = _ffn(x, l, P["ffn2_norm"], P["ffn2_wg"], P["ffn2_wu"], P["ffn2_wd"])
    h_last = h.reshape(nseq, rs, D_MODEL)[:, rs - 1]
    return x, s_t, h_last, conv_buf, kt_all, vt_all


def _chunk_sum_matrix(tm):
    i = lax.broadcasted_iota(jnp.int32, (tm, tm), 0)
    j = lax.broadcasted_iota(jnp.int32, (tm, tm), 1)
    same = (i // CHUNK) == (j // CHUNK)
    return jnp.concatenate([same & (j <= i), same], axis=0).astype(BF16)


def _lower_ones(n, strict):
    i = lax.broadcasted_iota(jnp.int32, (n, n), 0)
    j = lax.broadcasted_iota(jnp.int32, (n, n), 1)
    return ((j < i) if strict else (j <= i)).astype(BF16)


def kernel(x_prompt, x_sample, mem_prompt, cache_sb_k, cache_sb_v, state_rwkv, state_shift, state_conv,
           cache_mem_k, cache_mem_v, ffn1_norm, ffn1_wg, ffn1_wu, ffn1_wd, mix_norm, w_in, w_out, rw_mu_rkv,
           rw_mu_wag, rw_w0, rw_w1, rw_w2, rw_a0, rw_a1, rw_a2, rw_g1, rw_g2, rw_k_k, rw_k_a, rw_r_k, rw_ln_w,
           rw_ln_b, sb_q_norm, sb_k_norm, cv_w, x_norm, mem_norm, x_wq, x_wk, x_wv, x_wo, x_q_norm, x_k_norm,
           ffn2_norm, ffn2_wg, ffn2_wu, ffn2_wd):
    depth = w_in.shape[0]
    bp, tp, _ = x_prompt.shape
    bs, ts, _ = x_sample.shape
    n_mem = mem_prompt.shape[1]
    plen = cache_sb_k.shape[2]

    def vec(a):
        return a.reshape(a.shape[0], 1, -1)

    def per_head(a):
        return jnp.tile(a, (1, SB_HEADS))

    w_in_b = w_in.astype(BF16)
    P = dict(
        ffn1_norm=vec(ffn1_norm), ffn1_wg=ffn1_wg.astype(BF16), ffn1_wu=ffn1_wu.astype(BF16),
        ffn1_wd=ffn1_wd.astype(BF16),
        mix_norm=vec(mix_norm), w_rq=w_in_b[:, :, 0:_K0],
        w_kvt=jnp.swapaxes(w_in_b[:, :, _K0:_C0], 1, 2), w_cv=w_in_b[:, :, _C0:],
        w_out=w_out.astype(BF16),
        rw_mu_rkv=rw_mu_rkv.reshape(depth, 1, _RKV), rw_mu_wag=rw_mu_wag,
        rw_w0=vec(rw_w0), rw_w1=rw_w1.astype(BF16), rw_w2=rw_w2.astype(BF16),
        rw_a0=vec(rw_a0), rw_a1=rw_a1.astype(BF16), rw_a2=rw_a2.astype(BF16),
        rw_g1=rw_g1.astype(BF16), rw_g2=rw_g2.astype(BF16),
        rw_k_k=vec(rw_k_k), rw_k_a=vec(rw_k_a), rw_r_k=rw_r_k.reshape(depth, 1, RW_DIM),
        rw_ln_w=vec(rw_ln_w), rw_ln_b=vec(rw_ln_b),
        sb_q_norm=vec(per_head(sb_q_norm)), sb_k_norm_col=per_head(sb_k_norm).reshape(depth, SB_DIM, 1), cv_w=cv_w,
        x_norm=vec(x_norm), mem_norm=vec(mem_norm),
        x_wq=x_wq.astype(BF16), x_wk=x_wk.astype(BF16), x_wv=x_wv.astype(BF16), x_wo=x_wo.astype(BF16),
        x_q_norm=vec(x_q_norm), x_k_norm=vec(x_k_norm),
        ffn2_norm=vec(ffn2_norm), ffn2_wg=ffn2_wg.astype(BF16), ffn2_wu=ffn2_wu.astype(BF16),
        ffn2_wd=ffn2_wd.astype(BF16),
    )
    C = {}
    for tm in {min(RWKV_TM, bp * tp), min(RWKV_TM, bs * ts)}:
        C["cumtot%d" % tm] = _chunk_sum_matrix(tm)
    for size in {min(SB_TK, tp), min(SB_TK, ts), SB_TK}:
        C["low%d" % size] = jnp.tile(_lower_ones(size, True), (2, 1))

    dt = x_prompt.dtype
    zero_s = jnp.zeros((bp, RW_HEADS, HEAD_DIM, HEAD_DIM), state_rwkv.dtype)
    zero_shift = jnp.zeros((bp, 1, D_MODEL), dt)
    zero_rkv = jnp.zeros((bp, 1, _RKV), dt)
    zero_conv = jnp.zeros((bp, CONV_W - 1, CONV_DIM), dt)
    past_kt = jnp.transpose(cache_sb_k, (0, 1, 3, 4, 2)).reshape(depth, bs, SB_DIM, plen)
    past_vt = jnp.transpose(cache_sb_v, (0, 1, 3, 4, 2)).reshape(depth, bs, SB_DIM, plen)
    cmk = cache_mem_k.reshape(depth, bs, n_mem, D_MODEL)
    cmv = cache_mem_v.reshape(depth, bs, n_mem, D_MODEL)

    yp = x_prompt.reshape(bp * tp, D_MODEL)
    ys = x_sample.reshape(bs * ts, D_MODEL)
    mem = mem_prompt.reshape(bp * n_mem, D_MODEL)
    pkt = jnp.zeros((depth, bp, SB_DIM, tp), dt)
    pvt = jnp.zeros((depth, bp, SB_DIM, tp), dt)
    skt = jnp.zeros((depth, 1, SB_DIM, bs * ts), dt)
    svt = jnp.zeros((depth, 1, SB_DIM, bs * ts), dt)
    pmk, pmv, pmk2, pmv2 = _memory_kv(mem, P)
    pmk = pmk.reshape(depth, bp, n_mem, X_HEADS, X_HEAD_DIM)
    pmv = pmv.reshape(depth, bp, n_mem, X_HEADS, X_HEAD_DIM)
    pmk2 = pmk2.reshape(depth, bp, n_mem, D_MODEL)
    pmv2 = pmv2.reshape(depth, bp, n_mem, D_MODEL)
    pS, psh, pcv = [], [], []
    sS, ssh, scv = [], [], []
    for l in range(depth):
        yp, s_t, sh, cbuf, pkt, pvt = _layer(yp, l, bp, pmk2, pmv2, l, None, None, zero_s, zero_shift, zero_rkv,
                                             zero_conv, pkt, pvt, P, C)
        pS.append(s_t); psh.append(sh); pcv.append(cbuf)
        shift_l = state_shift[l]
        rkvprev = _state_proj(shift_l, l, P["w_rq"]).reshape(bs, 1, _RKV)
        ys, s_t, sh, cbuf, skt, svt = _layer(ys, l, bs, cmk, cmv, l, past_kt, past_vt, state_rwkv[l],
                                             shift_l.reshape(bs, 1, D_MODEL), rkvprev, state_conv[l], skt, svt, P, C)
        sS.append(s_t); ssh.append(sh); scv.append(cbuf)

    def stack(xs, shape):
        return jnp.stack(xs).reshape((depth,) + shape)

    def prompt_kv(a):
        return jnp.transpose(a.reshape(depth, bp, SB_HEADS, HEAD_DIM, tp), (0, 1, 4, 2, 3))

    def sample_kv(a):
        return jnp.transpose(a.reshape(depth, SB_HEADS, HEAD_DIM, bs, ts), (0, 3, 4, 1, 2))

    return (
        yp.reshape(bp, tp, D_MODEL), ys.reshape(bs, ts, D_MODEL),
        prompt_kv(pkt), prompt_kv(pvt),
        stack(pS, (bp, RW_HEADS, HEAD_DIM, HEAD_DIM)), stack(psh, (bp, D_MODEL)),
        stack(pcv, (bp, CONV_W - 1, CONV_DIM)),
        pmk, pmv,
        sample_kv(skt), sample_kv(svt),
        stack(sS, (bs, RW_HEADS, HEAD_DIM, HEAD_DIM)), stack(ssh, (bs, D_MODEL)),
        stack(scv, (bs, CONV_W - 1, CONV_DIM)),
    )
```

```python
import functools

import jax
import jax.numpy as jnp
from jax import lax
from jax.experimental import pallas as pl
from jax.experimental.pallas import tpu as pltpu

F32 = jnp.float32
BF16 = jnp.bfloat16

D_MODEL = 1024
HEAD_DIM = 64
RW_HEADS = 6
SB_HEADS = 6
RW_DIM = RW_HEADS * HEAD_DIM
SB_DIM = SB_HEADS * HEAD_DIM
CONV_DIM = 256
CONV_W = 3
FFN_DIM = 2816
X_HEADS = 4
X_HEAD_DIM = D_MODEL // X_HEADS
RMS_EPS = 1e-6
GN_EPS = 64e-5
L2_EPS = 1e-12
LOG2E = 1.4426950408889634

CHUNK = 64
LANES = 128
SUBLANES = 8
PAIRS = RW_DIM // LANES
VMEM_LIMIT = 56 * 1024 * 1024

FFN_TM = 1024
FFN_TF = 256
MIX_TM = 512
RWKV_TM = 256
SB_TQ = 128
SB_TK = 256
SB_CHAINS = 2
XA_TQ = 512
MEM_TM = 512

_RKV = 3 * RW_DIM
_Q0 = _RKV
_K0 = _Q0 + SB_DIM
_V0 = _K0 + SB_DIM
_C0 = _V0 + SB_DIM


def _dot(a, b):
    return jnp.dot(a, b, preferred_element_type=F32)


def _dot_nt(a, b):
    return lax.dot_general(a, b, (((1,), (1,)), ((), ())), preferred_element_type=F32)


def _dot_tn(a, b):
    return lax.dot_general(a, b, (((0,), (0,)), ((), ())), preferred_element_type=F32)


def _split(x, terms):
    parts = []
    rem = x
    for t in range(terms):
        p = rem.astype(BF16)
        parts.append(p)
        if t + 1 < terms:
            rem = rem - p.astype(F32)
    return parts


def _dot_exact_rhs(x, m, terms=3):
    acc = None
    for p in _split(x, terms):
        y = _dot(p, m)
        acc = y if acc is None else acc + y
    return acc


def _dot_exact_lhs(m, x, terms=3):
    acc = None
    for p in _split(x, terms):
        y = _dot(m, p)
        acc = y if acc is None else acc + y
    return acc


def _head_sums_lanes(x):
    lane_lo = _lane_lo()
    outs = []
    for p in range(x.shape[1] // LANES):
        xp = x[:, p * LANES:(p + 1) * LANES]
        lo = jnp.sum(jnp.where(lane_lo, xp, 0.0), axis=1, keepdims=True)
        hi = jnp.sum(jnp.where(lane_lo, 0.0, xp), axis=1, keepdims=True)
        outs.append(jnp.where(lane_lo, lo, hi))
    return jnp.concatenate(outs, axis=1)


def _rms(x, g):
    return x * lax.rsqrt(jnp.mean(x * x, axis=-1, keepdims=True) + RMS_EPS) * g


def _sigmoid(x):
    return 1.0 / (1.0 + jnp.exp(-x))


def _softplus(y):
    return jnp.maximum(y, 0.0) + jnp.log1p(jnp.exp(-jnp.abs(y)))


def _params(sem):
    return pltpu.CompilerParams(dimension_semantics=sem, vmem_limit_bytes=VMEM_LIMIT)


def _lspec(l, tail):
    nz = (0,) * len(tail)
    return pl.BlockSpec((None,) + tuple(tail), lambda *_: (l,) + nz)


def _lane_lo():
    return lax.broadcasted_iota(jnp.int32, (1, LANES), 1) < HEAD_DIM


def _pair_stack(xp, lane_lo):
    return jnp.concatenate([jnp.where(lane_lo, xp, 0.0), jnp.where(lane_lo, 0.0, xp)], axis=0)


def _ffn_body(x_ref, g_ref, wg_ref, wu_ref, wd_ref, o_ref):
    x = x_ref[...]
    h = _rms(x, g_ref[...]).astype(BF16)
    acc = None
    for f in range(FFN_DIM // FFN_TF):
        fs = slice(f * FFN_TF, (f + 1) * FFN_TF)
        gate = _dot(h, wg_ref[:, fs])
        up = _dot(h, wu_ref[:, fs])
        act = (gate * _sigmoid(gate) * up).astype(BF16)
        part = _dot(act, wd_ref[fs, :])
        acc = part if acc is None else acc + part
    o_ref[...] = x + 0.5 * acc


def _ffn(x, l, norm, wg, wu, wd):
    n = x.shape[0]
    tm = min(FFN_TM, n)
    once = pl.Buffered(1)
    return pl.pallas_call(
        _ffn_body,
        grid=(n // tm,),
        in_specs=[
            pl.BlockSpec((tm, D_MODEL), lambda i: (i, 0)),
            _lspec(l, (1, D_MODEL)),
            pl.BlockSpec((None, D_MODEL, FFN_DIM), lambda i: (l, 0, 0), pipeline_mode=once),
            pl.BlockSpec((None, D_MODEL, FFN_DIM), lambda i: (l, 0, 0), pipeline_mode=once),
            pl.BlockSpec((None, FFN_DIM, D_MODEL), lambda i: (l, 0, 0), pipeline_mode=once),
        ],
        out_specs=pl.BlockSpec((tm, D_MODEL), lambda i: (i, 0)),
        out_shape=jax.ShapeDtypeStruct((n, D_MODEL), F32),
        compiler_params=_params(("arbitrary",)),
        name="ffn",
    )(x, norm, wg, wu, wd)


def _mix_body(x_ref, g_ref, wrq_ref, wkv_ref, wcv_ref, qg_ref, kg_ref, cw_ref, c0_ref, kin_ref, vin_ref,
              h_ref, rkv_ref, q_ref, ocv_ref, cbuf_ref, kt_ref, vt_ref, tail_scr, *, tm, rs):
    del kin_ref, vin_ref
    i = pl.program_id(0)
    h = _rms(x_ref[...], g_ref[...])
    h_ref[...] = h
    hb = h.astype(BF16)
    rkv_ref[...] = _dot(hb, wrq_ref[:, 0:_RKV])
    sq = _dot(hb, wrq_ref[:, _RKV:_RKV + SB_DIM])
    ms = _head_sums_lanes(sq * sq) * (1.0 / HEAD_DIM)
    q_ref[...] = sq * lax.rsqrt(ms + RMS_EPS) * qg_ref[...] * (HEAD_DIM ** -0.5 * LOG2E)

    kt = _dot_nt(wkv_ref[0:SB_DIM, :], hb)
    k3 = kt.reshape(SB_HEADS, HEAD_DIM, tm)
    kms = jnp.mean(k3 * k3, axis=1, keepdims=True)
    kt_ref[...] = (k3 * lax.rsqrt(kms + RMS_EPS)).reshape(SB_DIM, tm) * kg_ref[...]
    vt_ref[...] = _dot_nt(wkv_ref[SB_DIM:2 * SB_DIM, :], hb)

    cb = _dot(hb, wcv_ref[:, 0:CONV_DIM])
    u = _dot(hb, wcv_ref[:, CONV_DIM:2 * CONV_DIM]) * _dot(hb, wcv_ref[:, 2 * CONV_DIM:3 * CONV_DIM])
    w0 = cw_ref[0:1, :]
    w1 = cw_ref[1:2, :]
    w2 = cw_ref[2:3, :]
    seg = min(tm, rs)
    nseg = tm // seg
    if rs >= tm:
        tiles_per_seq = rs // tm

        @pl.when(i % tiles_per_seq == 0)
        def _():
            tail_scr[SUBLANES - 2:SUBLANES, :] = c0_ref[0]

    row = lax.broadcasted_iota(jnp.int32, (seg, CONV_DIM), 0)
    for s in range(nseg):
        us = u[s * seg:(s + 1) * seg]
        if rs >= tm:
            b0 = tail_scr[SUBLANES - 2:SUBLANES - 1, :]
            b1 = tail_scr[SUBLANES - 1:SUBLANES, :]
        else:
            b0 = c0_ref[s, 0:1, :]
            b1 = c0_ref[s, 1:2, :]
        p1 = jnp.where(row == 0, b1, pltpu.roll(us, 1, 0))
        p2 = jnp.where(row == 0, b0, jnp.where(row == 1, b1, pltpu.roll(us, 2, 0)))
        y = p2 * w0 + p1 * w1 + us * w2
        ocv_ref[s * seg:(s + 1) * seg, :] = cb[s * seg:(s + 1) * seg] * y
        tail_scr[...] = us[seg - SUBLANES:seg]
        cbuf_ref[s] = tail_scr[SUBLANES - 2:SUBLANES, :]


def _mix_proj(x, l, rs, P, C, conv0, kt_all, vt_all):
    n = x.shape[0]
    tm = min(MIX_TM, n)
    nseq = n // rs
    if rs >= tm:
        tps = rs // tm
        nsb = 1
        smap = lambda i: (i // tps, 0, 0)
        tmap = lambda i: (l, i // tps, 0, i % tps)
    else:
        nsb = tm // rs
        smap = lambda i: (i, 0, 0)
        tmap = lambda i: (l, 0, 0, i)
    row = lambda i: (i, 0)
    tspec = pl.BlockSpec((None, None, SB_DIM, tm), tmap)
    return pl.pallas_call(
        functools.partial(_mix_body, tm=tm, rs=rs),
        grid=(n // tm,),
        in_specs=[
            pl.BlockSpec((tm, D_MODEL), row),
            _lspec(l, (1, D_MODEL)),
            _lspec(l, (D_MODEL, _RKV + SB_DIM)),
            _lspec(l, (2 * SB_DIM, D_MODEL)),
            _lspec(l, (D_MODEL, 3 * CONV_DIM)),
            _lspec(l, (1, SB_DIM)),
            _lspec(l, (SB_DIM, 1)),
            _lspec(l, (CONV_W, CONV_DIM)),
            pl.BlockSpec((nsb, CONV_W - 1, CONV_DIM), smap),
            pl.BlockSpec(memory_space=pl.ANY),
            pl.BlockSpec(memory_space=pl.ANY),
        ],
        out_specs=[
            pl.BlockSpec((tm, D_MODEL), row),
            pl.BlockSpec((tm, _RKV), row),
            pl.BlockSpec((tm, SB_DIM), row),
            pl.BlockSpec((tm, CONV_DIM), row),
            pl.BlockSpec((nsb, CONV_W - 1, CONV_DIM), smap),
            tspec,
            tspec,
        ],
        out_shape=[
            jax.ShapeDtypeStruct((n, D_MODEL), F32),
            jax.ShapeDtypeStruct((n, _RKV), F32),
            jax.ShapeDtypeStruct((n, SB_DIM), F32),
            jax.ShapeDtypeStruct((n, CONV_DIM), F32),
            jax.ShapeDtypeStruct((nseq, CONV_W - 1, CONV_DIM), F32),
            jax.ShapeDtypeStruct(kt_all.shape, F32),
            jax.ShapeDtypeStruct(vt_all.shape, F32),
        ],
        input_output_aliases={9: 5, 10: 6},
        scratch_shapes=[pltpu.VMEM((SUBLANES, CONV_DIM), F32)],
        compiler_params=_params(("arbitrary",)),
        name="mix_proj",
    )(x, P["mix_norm"], P["w_rq"], P["w_kvt"], P["w_cv"], P["sb_q_norm"], P["sb_k_norm_col"],
      P["cv_w"], conv0, kt_all, vt_all)


def _state_proj_body(s_ref, w_ref, o_ref):
    o_ref[...] = _dot(s_ref[...].astype(BF16), w_ref[:, 0:_RKV])


def _state_proj(shift0, l, w_rq):
    b = shift0.shape[0]
    return pl.pallas_call(
        _state_proj_body,
        grid=(1,),
        in_specs=[pl.BlockSpec((b, D_MODEL), lambda i: (0, 0)), _lspec(l, (D_MODEL, _RKV + SB_DIM))],
        out_specs=pl.BlockSpec((b, _RKV), lambda i: (0, 0)),
        out_shape=jax.ShapeDtypeStruct((b, _RKV), F32),
        compiler_params=_params(("arbitrary",)),
        name="state_proj",
    )(shift0, w_rq)


def _unit_lower_inverses(lmats):
    n = lmats[0].shape[0]
    eye = (lax.broadcasted_iota(jnp.int32, (n, n), 0) == lax.broadcasted_iota(jnp.int32, (n, n), 1)).astype(F32)
    ps = [eye + lm for lm in lmats]
    lks = []
    for lm in lmats:
        lb = lm.astype(BF16)
        lks.append(_dot(lb, lb))
    power = 2
    while power < CHUNK:
        for t in range(len(lmats)):
            lkb = lks[t].astype(BF16)
            if 2 * power < CHUNK:
                both = _dot(lkb, jnp.concatenate([lks[t], ps[t]], axis=1).astype(BF16))
                lks[t] = both[:, 0:n]
                ps[t] = ps[t] + both[:, n:2 * n]
            else:
                ps[t] = ps[t] + _dot(lkb, ps[t].astype(BF16))
        power *= 2
    return ps


def _state_to_pairs(s_ref, sq):
    zero = jnp.zeros((HEAD_DIM, HEAD_DIM), F32)
    out = []
    for p in range(PAIRS):
        top = jnp.concatenate([s_ref[sq, 2 * p], zero], axis=1)
        bot = jnp.concatenate([zero, s_ref[sq, 2 * p + 1]], axis=1)
        out.append(jnp.concatenate([top, bot], axis=0))
    return out


def _pairs_to_state(sbd, sout_ref, sq):
    for p in range(PAIRS):
        sout_ref[sq, 2 * p] = sbd[p][0:HEAD_DIM, 0:HEAD_DIM]
        sout_ref[sq, 2 * p + 1] = sbd[p][HEAD_DIM:LANES, HEAD_DIM:LANES]


def _rwkv_body(h_ref, rkv_ref, sh0_ref, rp0_ref, s0_ref,
               murkv_ref, muwag_ref, w0_ref, w1_ref, w2_ref, a0_ref, a1_ref, a2_ref, g1_ref, g2_ref,
               kk_ref, ka_ref, rk_ref, lnw_ref, lnb_ref, csum_ref,
               o_ref, sout_ref, hl_scr, rl_scr, s_scr, *, tm, rs):
    i = pl.program_id(0)
    seg = min(tm, rs)
    nseg = tm // seg
    nchunk = tm // CHUNK
    carried = rs >= tm
    if carried:
        tiles_per_seq = rs // tm

        @pl.when(i % tiles_per_seq == 0)
        def _():
            hl_scr[SUBLANES - 1:SUBLANES, :] = sh0_ref[0]
            rl_scr[SUBLANES - 1:SUBLANES, :] = rp0_ref[0]
            init = _state_to_pairs(s0_ref, 0)
            for p in range(PAIRS):
                s_scr[p] = init[p]

    h = h_ref[...]
    rkv = rkv_ref[...]
    if carried:
        hprev_rows = hl_scr[SUBLANES - 1:SUBLANES, :]
        rprev_rows = rl_scr[SUBLANES - 1:SUBLANES, :]
    else:
        hprev_rows = jnp.concatenate(
            [jnp.broadcast_to(sh0_ref[s], (seg, D_MODEL)) for s in range(nseg)], axis=0)
        rprev_rows = jnp.concatenate(
            [jnp.broadcast_to(rp0_ref[s], (seg, _RKV)) for s in range(nseg)], axis=0)
    first_h = (lax.broadcasted_iota(jnp.int32, (tm, D_MODEL), 0) & (seg - 1)) == 0
    first_r = (lax.broadcasted_iota(jnp.int32, (tm, _RKV), 0) & (seg - 1)) == 0
    hp = jnp.where(first_h, hprev_rows, pltpu.roll(h, 1, 0))
    rp = jnp.where(first_r, rprev_rows, pltpu.roll(rkv, 1, 0))
    if carried:
        hl_scr[...] = h[tm - SUBLANES:tm]
        rl_scr[...] = rkv[tm - SUBLANES:tm]

    rkv = rkv + murkv_ref[...] * (rp - rkv)
    r = rkv[:, 0:RW_DIM]
    k = rkv[:, RW_DIM:2 * RW_DIM]
    v = rkv[:, 2 * RW_DIM:3 * RW_DIM]
    dx = hp - h
    xw = (h + muwag_ref[0:1, :] * dx).astype(BF16)
    xa = (h + muwag_ref[1:2, :] * dx).astype(BF16)
    xg = (h + muwag_ref[2:3, :] * dx).astype(BF16)
    wl = w0_ref[...] + _dot(jnp.tanh(_dot(xw, w1_ref[...])).astype(BF16), w2_ref[...])
    w = -_softplus(-wl) - 0.5
    logdecay = -jnp.exp(w)
    a = _sigmoid(a0_ref[...] + _dot(_dot(xa, a1_ref[...]).astype(BF16), a2_ref[...]))
    gate = _dot(_sigmoid(_dot(xg, g1_ref[...])).astype(BF16), g2_ref[...])
    kk = k * kk_ref[...]
    kk = kk * lax.rsqrt(_head_sums_lanes(kk * kk) + L2_EPS)
    k = k * (1.0 + (a - 1.0) * ka_ref[...])
    b = kk * a
    bonus = _head_sums_lanes(r * k * rk_ref[...]) * v

    ci = lax.broadcasted_iota(jnp.int32, (LANES, LANES), 0)
    cj = lax.broadcasted_iota(jnp.int32, (LANES, LANES), 1)
    strict = cj < ci
    incl = cj <= ci
    lane_lo = _lane_lo()

    sums = _dot_exact_lhs(csum_ref[...], logdecay, terms=2)
    cum = sums[0:tm]
    tot = sums[tm:2 * tm]
    g_incl = jnp.exp(cum)
    g_prev = jnp.exp(cum - logdecay)
    g_inv = jnp.exp(-cum)
    g_rem = jnp.exp(tot - cum)
    g_tot = jnp.exp(tot)
    rq = r * g_incl
    kq = k * g_inv
    bq = b * g_inv
    aq = -kk * g_prev
    kz = k * g_rem
    bz = b * g_rem
    items = [(c, p) for c in range(nchunk) for p in range(PAIRS)]

    def stacked(x, c, p):
        return _pair_stack(x[c * CHUNK:(c + 1) * CHUNK, p * LANES:(p + 1) * LANES], lane_lo).astype(BF16)

    work = [dict(aq=stacked(aq, c, p), rq=stacked(rq, c, p), v=stacked(v, c, p),
                 kzbz=jnp.concatenate([stacked(kz, c, p), stacked(bz, c, p)], axis=0),
                 gtot=g_tot[c * CHUNK:c * CHUNK + 1, p * LANES:(p + 1) * LANES]) for c, p in items]
    quads = [_dot_nt(jnp.concatenate([wk["aq"], wk["rq"]], axis=0),
                     jnp.concatenate([stacked(kq, c, p), stacked(bq, c, p)], axis=0))
             for wk, (c, p) in zip(work, items)]
    lmats = []
    for wk, quad in zip(work, quads):
        wk["m_ak"] = jnp.where(strict, quad[0:LANES, 0:LANES], 0.0).astype(BF16)
        lmats.append(jnp.where(strict, quad[0:LANES, LANES:2 * LANES], 0.0))
        wk["mr"] = jnp.concatenate([jnp.where(incl, quad[LANES:2 * LANES, 0:LANES], 0.0),
                                    jnp.where(incl, quad[LANES:2 * LANES, LANES:2 * LANES], 0.0)],
                                   axis=1).astype(BF16)
    for wk in work:
        wk["x1"] = _dot(wk["m_ak"], wk["v"])
    tinvs = _unit_lower_inverses(lmats)
    for wk, tinv in zip(work, tinvs):
        wu = _dot(tinv.astype(BF16), jnp.concatenate([wk["aq"], wk["x1"].astype(BF16)], axis=1))
        wk["wr"] = jnp.concatenate([wu[:, 0:LANES].astype(BF16), wk["rq"]], axis=0)
        wk["u0"] = wu[:, LANES:2 * LANES]

    if carried:
        sbd = [s_scr[p] for p in range(PAIRS)]
    o_chunks = []
    for c in range(nchunk):
        sq = (c * CHUNK) // seg
        if not carried:
            sbd = _state_to_pairs(s0_ref, sq)
        wks = work[c * PAIRS:(c + 1) * PAIRS]
        ys = [_dot_nt(wks[p]["wr"], sbd[p].astype(BF16)) for p in range(PAIRS)]
        vus = [jnp.concatenate([wks[p]["v"], (ys[p][0:LANES] + wks[p]["u0"]).astype(BF16)], axis=0)
               for p in range(PAIRS)]
        sbd = [sbd[p] * wks[p]["gtot"] + _dot_tn(vus[p], wks[p]["kzbz"]) for p in range(PAIRS)]
        o_ss = [ys[p][LANES:2 * LANES] + _dot(wks[p]["mr"], vus[p]) for p in range(PAIRS)]
        if not carried:
            _pairs_to_state(sbd, sout_ref, sq)
        o_chunks.append(jnp.concatenate([o_s[0:CHUNK] + o_s[CHUNK:2 * CHUNK] for o_s in o_ss], axis=1))
    if carried:
        for p in range(PAIRS):
            s_scr[p] = sbd[p]
        _pairs_to_state(sbd, sout_ref, 0)
    o = jnp.concatenate(o_chunks, axis=0)
    mean = _head_sums_lanes(o) * (1.0 / HEAD_DIM)
    d = o - mean
    var = _head_sums_lanes(d * d) * (1.0 / HEAD_DIM)
    on = d * lax.rsqrt(var + GN_EPS) * lnw_ref[...] + lnb_ref[...]
    o_ref[...] = (on + bonus) * gate


def _rwkv(h, rkv, l, rs, shift0, rkvprev0, s0, P, C):
    n = h.shape[0]
    tm = min(RWKV_TM, n)
    nseq = n // rs
    if rs >= tm:
        tps = rs // tm
        nsb = 1
        smap3 = lambda i: (i // tps, 0, 0)
        smap4 = lambda i: (i // tps, 0, 0, 0)
    else:
        nsb = tm // rs
        smap3 = lambda i: (i, 0, 0)
        smap4 = lambda i: (i, 0, 0, 0)
    row = lambda i: (i, 0)
    lora_w, lora_a, lora_g = P["rw_w1"].shape[-1], P["rw_a1"].shape[-1], P["rw_g1"].shape[-1]
    return pl.pallas_call(
        functools.partial(_rwkv_body, tm=tm, rs=rs),
        grid=(n // tm,),
        in_specs=[
            pl.BlockSpec((tm, D_MODEL), row),
            pl.BlockSpec((tm, _RKV), row),
            pl.BlockSpec((nsb, 1, D_MODEL), smap3),
            pl.BlockSpec((nsb, 1, _RKV), smap3),
            pl.BlockSpec((nsb, RW_HEADS, HEAD_DIM, HEAD_DIM), smap4),
            _lspec(l, (1, _RKV)),
            _lspec(l, (3, D_MODEL)),
            _lspec(l, (1, RW_DIM)),
            _lspec(l, (D_MODEL, lora_w)),
            _lspec(l, (lora_w, RW_DIM)),
            _lspec(l, (1, RW_DIM)),
            _lspec(l, (D_MODEL, lora_a)),
            _lspec(l, (lora_a, RW_DIM)),
            _lspec(l, (D_MODEL, lora_g)),
            _lspec(l, (lora_g, RW_DIM)),
            _lspec(l, (1, RW_DIM)),
            _lspec(l, (1, RW_DIM)),
            _lspec(l, (1, RW_DIM)),
            _lspec(l, (1, RW_DIM)),
            _lspec(l, (1, RW_DIM)),
            pl.BlockSpec((2 * tm, tm), lambda i: (0, 0)),
        ],
        out_specs=[
            pl.BlockSpec((tm, RW_DIM), row),
            pl.BlockSpec((nsb, RW_HEADS, HEAD_DIM, HEAD_DIM), smap4),
        ],
        out_shape=[
            jax.ShapeDtypeStruct((n, RW_DIM), F32),
            jax.ShapeDtypeStruct((nseq, RW_HEADS, HEAD_DIM, HEAD_DIM), F32),
        ],
        scratch_shapes=[
            pltpu.VMEM((SUBLANES, D_MODEL), F32),
            pltpu.VMEM((SUBLANES, _RKV), F32),
            pltpu.VMEM((PAIRS, LANES, LANES), F32),
        ],
        compiler_params=_params(("arbitrary",)),
        name="rwkv",
    )(h, rkv, shift0, rkvprev0, s0,
      P["rw_mu_rkv"], P["rw_mu_wag"], P["rw_w0"], P["rw_w1"], P["rw_w2"], P["rw_a0"], P["rw_a1"],
      P["rw_a2"], P["rw_g1"], P["rw_g2"], P["rw_k_k"], P["rw_k_a"], P["rw_r_k"], P["rw_ln_w"],
      P["rw_ln_b"], C["cumtot%d" % tm])


def _sb_log_rest(z, mask):
    log_rest = jnp.minimum(-z, 0.0) - jnp.log2(1.0 + jnp.exp2(-jnp.abs(z)))
    if mask is not None:
        log_rest = jnp.where(mask, log_rest, 0.0)
    return log_rest


def _sb_weights(z, incl, carry, mask):
    wts = jnp.exp2(z + incl + carry)
    if mask is not None:
        wts = jnp.where(mask, wts, 0.0)
    return wts.astype(BF16), carry + incl[:, 0:1]


def _sb_body(*refs, tq, nsub, nseqs, tkd, n_past, tkp):
    if n_past:
        q_ref, kc_ref, vc_ref, kp_ref, vp_ref, lowd_ref, lowp_ref, o_ref = refs
    else:
        q_ref, kc_ref, vc_ref, lowd_ref, o_ref = refs
    qi = pl.program_id(1)
    lane_lo = _lane_lo()
    rows = SB_HEADS * tq
    chains = [(g, s) for g in range(nseqs) for s in range(nsub)]
    qs = [[_pair_stack(q_ref[c * tq:(c + 1) * tq, p * LANES:(p + 1) * LANES], lane_lo).astype(BF16)
           for p in range(PAIRS)] for c in range(len(chains))]

    def block(ref, g, p, off, tk):
        cols = slice(None) if ref.shape[2] == tk else pl.ds(off, tk)
        return ref[g, p * LANES:(p + 1) * LANES, cols].astype(BF16)

    def step(k_ref, v_ref, off, tk, lower, st, masks, widths):
        kbs = [[block(k_ref, g, p, off, tk) for p in range(PAIRS)] for g in range(nseqs)]
        vbs = [[block(v_ref, g, p, off, tk) for p in range(PAIRS)] for g in range(nseqs)]
        zs = [jnp.concatenate([_dot(qs[c][p], kbs[g][p][:, 0:widths[c]]) for p in range(PAIRS)], axis=0)
              for c, (g, _) in enumerate(chains)]
        incls = []
        for c in range(len(chains)):
            w = widths[c]
            hi, lo = _split(_sb_log_rest(zs[c], masks[c]), 2)
            low = lower if w == tk else jnp.concatenate([lower[0:w, 0:w], lower[tk:tk + w, 0:w]], axis=0)
            incls.append(_dot(jnp.concatenate([hi, lo], axis=1), low))
        out = []
        for c, (g, _) in enumerate(chains):
            wts, carry = _sb_weights(zs[c], incls[c], st[2 * c], masks[c])
            pv = jnp.concatenate(
                [_dot_nt(wts[2 * tq * p:2 * tq * (p + 1)], vbs[g][p][:, 0:widths[c]]) for p in range(PAIRS)], axis=0)
            out += [carry, st[2 * c + 1] + pv]
        return tuple(out)

    lowd = lowd_ref[...]
    q0 = qi * (tq * nsub)
    jd = q0 // tkd
    offd = pl.multiple_of(jd * tkd, tkd)
    diag_w = [min(tkd, (s + 1) * tq) for _, s in chains]
    masks = []
    for (_, s), w in zip(chains, diag_w):
        kpos = offd + lax.broadcasted_iota(jnp.int32, (rows, w), 1)
        rowq = lax.broadcasted_iota(jnp.int32, (rows, w), 0) & (tq - 1)
        masks.append(kpos < (q0 + s * tq + rowq))
    st = (jnp.zeros((rows, 1), F32), jnp.zeros((rows, LANES), F32)) * len(chains)
    st = step(kc_ref, vc_ref, offd, tkd, lowd, st, masks, diag_w)
    nomask = [None] * len(chains)
    full_d = [tkd] * len(chains)
    full_p = [tkp] * len(chains)

    def cur_step(jj, st):
        off = pl.multiple_of((jd - 1 - jj) * tkd, tkd)
        return step(kc_ref, vc_ref, off, tkd, lowd, st, nomask, full_d)

    st = lax.fori_loop(0, jd, cur_step, st)
    if n_past:
        lowp = lowp_ref[...]

        def past_step(jj, st):
            off = pl.multiple_of((n_past - 1 - jj) * tkp, tkp)
            return step(kp_ref, vp_ref, off, tkp, lowp, st, nomask, full_p)

        st = lax.fori_loop(0, n_past, past_step, st)
    for c in range(len(chains)):
        acc = st[2 * c + 1]
        for p in range(PAIRS):
            base = 2 * tq * p
            o_ref[c * tq:(c + 1) * tq, p * LANES:(p + 1) * LANES] = jnp.where(
                lane_lo, acc[base:base + tq], acc[base + tq:base + 2 * tq])


def _stick_breaking(q, kt, vt, kl, nseq, past_kt, past_vt, pl_, C):
    n = q.shape[0]
    t = n // nseq
    tq = min(SB_TQ, t)
    tkd = min(SB_TK, t)
    nsub = max(1, tkd // tq)
    nseqs = max(1, SB_CHAINS // nsub)
    nq = t // (tq * nsub)
    qmap = lambda b, i: (b * nq + i, 0)
    cmap = lambda b, i: (kl, b, 0, 0)
    in_specs = [
        pl.BlockSpec((tq * nsub * nseqs, SB_DIM), qmap),
        pl.BlockSpec((None, nseqs, SB_DIM, t), cmap),
        pl.BlockSpec((None, nseqs, SB_DIM, t), cmap),
    ]
    args = [q, kt, vt]
    n_past = 0
    if past_kt is not None:
        plen = past_kt.shape[3]
        n_past = plen // SB_TK
        pmap = lambda b, i: (pl_, b, 0, 0)
        in_specs += [pl.BlockSpec((None, nseqs, SB_DIM, plen), pmap),
                     pl.BlockSpec((None, nseqs, SB_DIM, plen), pmap)]
        args += [past_kt, past_vt]
    in_specs.append(pl.BlockSpec((2 * tkd, tkd), lambda b, i: (0, 0)))
    args.append(C["low%d" % tkd])
    if n_past:
        in_specs.append(pl.BlockSpec((2 * SB_TK, SB_TK), lambda b, i: (0, 0)))
        args.append(C["low%d" % SB_TK])
    return pl.pallas_call(
        functools.partial(_sb_body, tq=tq, nsub=nsub, nseqs=nseqs, tkd=tkd, n_past=n_past, tkp=SB_TK),
        grid=(nseq // nseqs, nq),
        in_specs=in_specs,
        out_specs=pl.BlockSpec((tq * nsub * nseqs, SB_DIM), qmap),
        out_shape=jax.ShapeDtypeStruct((n, SB_DIM), F32),
        compiler_params=_params(("arbitrary", "arbitrary")),
        name="stick_breaking",
    )(*args)


def _xattn_body(x_ref, orw_ref, osb_ref, ocv_ref, wout_ref, g_ref, wq_ref, qg_ref, mk_ref, mv_ref, wo_ref, o_ref,
                *, nseqs, t):
    x = x_ref[...]
    x = x + _dot(orw_ref[...].astype(BF16), wout_ref[0:RW_DIM, :])
    x = x + _dot(osb_ref[...].astype(BF16), wout_ref[RW_DIM:RW_DIM + SB_DIM, :])
    x = x + _dot(ocv_ref[...].astype(BF16), wout_ref[RW_DIM + SB_DIM:D_MODEL, :])
    hx = _rms(x, g_ref[...]).astype(BF16)
    qf = _dot(hx, wq_ref[...])
    heads = []
    for hh in range(X_HEADS):
        cs = slice(hh * X_HEAD_DIM, (hh + 1) * X_HEAD_DIM)
        qh = _rms(qf[:, cs], qg_ref[...]).astype(BF16)
        outs = []
        for g in range(nseqs):
            s = _dot_nt(qh[g * t:(g + 1) * t], mk_ref[g, :, cs].astype(BF16)) * (X_HEAD_DIM ** -0.5)
            s = jnp.exp(s - jnp.max(s, axis=-1, keepdims=True))
            attn = s / jnp.sum(s, axis=-1, keepdims=True)
            outs.append(_dot(attn.astype(BF16), mv_ref[g, :, cs].astype(BF16)))
        heads.append(outs[0] if nseqs == 1 else jnp.concatenate(outs, axis=0))
    o = jnp.concatenate(heads, axis=1).astype(BF16)
    o_ref[...] = x + _dot(o, wo_ref[...])


def _xattn(x, orw, osb, ocv, l, nseq, P, mem_k, mem_v, mem_l):
    n = x.shape[0]
    rs = n // nseq
    t = min(XA_TQ, rs)
    nseqs = max(1, XA_TQ // rs)
    tq = t * nseqs
    nq = rs // t
    nm = mem_k.shape[2]
    row = lambda b, i: (b * nq + i, 0)
    mmap = lambda b, i: (mem_l, b, 0, 0)
    return pl.pallas_call(
        functools.partial(_xattn_body, nseqs=nseqs, t=t),
        grid=(nseq // nseqs, nq),
        in_specs=[
            pl.BlockSpec((tq, D_MODEL), row),
            pl.BlockSpec((tq, RW_DIM), row),
            pl.BlockSpec((tq, SB_DIM), row),
            pl.BlockSpec((tq, CONV_DIM), row),
            _lspec(l, (D_MODEL, D_MODEL)),
            _lspec(l, (1, D_MODEL)),
            _lspec(l, (D_MODEL, D_MODEL)),
            _lspec(l, (1, X_HEAD_DIM)),
            pl.BlockSpec((None, nseqs, nm, D_MODEL), mmap),
            pl.BlockSpec((None, nseqs, nm, D_MODEL), mmap),
            _lspec(l, (D_MODEL, D_MODEL)),
        ],
        out_specs=pl.BlockSpec((tq, D_MODEL), row),
        out_shape=jax.ShapeDtypeStruct((n, D_MODEL), F32),
        compiler_params=_params(("arbitrary", "arbitrary")),
        name="xattn",
    )(x, orw, osb, ocv, P["w_out"], P["x_norm"], P["x_wq"], P["x_q_norm"], mem_k, mem_v, P["x_wo"])


def _memkv_body(m_ref, g_ref, wk_ref, wv_ref, kg_ref, mk4_ref, mv4_ref, mk_ref, mv_ref):
    m = _rms(m_ref[...], g_ref[...]).astype(BF16)
    kf = _dot(m, wk_ref[...])
    vf = _dot(m, wv_ref[...])
    mv_ref[...] = vf
    for hh in range(X_HEADS):
        cs = slice(hh * X_HEAD_DIM, (hh + 1) * X_HEAD_DIM)
        kn = _rms(kf[:, cs], kg_ref[...])
        mk_ref[:, cs] = kn
        mk4_ref[:, hh, :] = kn
        mv4_ref[:, hh, :] = vf[:, cs]


def _memory_kv(mem, P):
    n = mem.shape[0]
    depth = P["x_wk"].shape[0]
    tm = min(MEM_TM, n)
    lay = lambda l, i: (l, 0, 0)
    out4 = pl.BlockSpec((None, tm, X_HEADS, X_HEAD_DIM), lambda l, i: (l, i, 0, 0))
    out2 = pl.BlockSpec((None, tm, D_MODEL), lambda l, i: (l, i, 0))
    shape4 = jax.ShapeDtypeStruct((depth, n, X_HEADS, X_HEAD_DIM), F32)
    shape2 = jax.ShapeDtypeStruct((depth, n, D_MODEL), F32)
    return pl.pallas_call(
        _memkv_body,
        grid=(depth, n // tm),
        in_specs=[
            pl.BlockSpec((tm, D_MODEL), lambda l, i: (i, 0)),
            pl.BlockSpec((None, 1, D_MODEL), lay),
            pl.BlockSpec((None, D_MODEL, D_MODEL), lay),
            pl.BlockSpec((None, D_MODEL, D_MODEL), lay),
            pl.BlockSpec((None, 1, X_HEAD_DIM), lay),
        ],
        out_specs=[out4, out4, out2, out2],
        out_shape=[shape4, shape4, shape2, shape2],
        compiler_params=_params(("arbitrary", "arbitrary")),
        name="memory_kv",
    )(mem, P["mem_norm"], P["x_wk"], P["x_wv"], P["x_k_norm"])


def _layer(x, l, nseq, mem_k, mem_v, mem_l, past_kt, past_vt, s0, shift0, rkvprev0, conv0, kt_all, vt_all, P, C):
    n = x.shape[0]
    rs = n // nseq
    x = _ffn(x, l, P["ffn1_norm"], P["ffn1_wg"], P["ffn1_wu"], P["ffn1_wd"])
    h, rkv, q, o_cv, conv_buf, kt_all, vt_all = _mix_proj(x, l, rs, P, C, conv0, kt_all, vt_all)
    o_rw, s_t = _rwkv(h, rkv, l, rs, shift0, rkvprev0, s0, P, C)
    if kt_all.shape[1] == nseq:
        kt_cur, vt_cur, kl = kt_all, vt_all, l
    else:
        def per_seq(a):
            return a[l, 0].reshape(SB_DIM, nseq, rs).transpose(1, 0, 2)[None]
        kt_cur, vt_cur, kl = per_seq(kt_all), per_seq(vt_all), 0
    o_sb = _stick_breaking(q, kt_cur, vt_cur, kl, nseq, past_kt, past_vt, l, C)
    x = _xattn(x, o_rw, o_sb, o_cv, l, nseq, P, mem_k, mem_v, mem_l)
    x = _ffn(x, l, P["ffn2_norm"], P["ffn2_wg"], P["ffn2_wu"], P["ffn2_wd"])
    h_last = h.reshape(nseq, rs, D_MODEL)[:, rs - 1]
    return x, s_t, h_last, conv_buf, kt_all, vt_all


def _chunk_sum_matrix(tm):
    i = lax.broadcasted_iota(jnp.int32, (tm, tm), 0)
    j = lax.broadcasted_iota(jnp.int32, (tm, tm), 1)
    same = (i // CHUNK) == (j // CHUNK)
    return jnp.concatenate([same & (j <= i), same], axis=0).astype(BF16)


def _lower_ones(n, strict):
    i = lax.broadcasted_iota(jnp.int32, (n, n), 0)
    j = lax.broadcasted_iota(jnp.int32, (n, n), 1)
    return ((j < i) if strict else (j <= i)).astype(BF16)


def kernel(x_prompt, x_sample, mem_prompt, cache_sb_k, cache_sb_v, state_rwkv, state_shift, state_conv,
           cache_mem_k, cache_mem_v, ffn1_norm, ffn1_wg, ffn1_wu, ffn1_wd, mix_norm, w_in, w_out, rw_mu_rkv,
           rw_mu_wag, rw_w0, rw_w1, rw_w2, rw_a0, rw_a1, rw_a2, rw_g1, rw_g2, rw_k_k, rw_k_a, rw_r_k, rw_ln_w,
           rw_ln_b, sb_q_norm, sb_k_norm, cv_w, x_norm, mem_norm, x_wq, x_wk, x_wv, x_wo, x_q_norm, x_k_norm,
           ffn2_norm, ffn2_wg, ffn2_wu, ffn2_wd):
    depth = w_in.shape[0]
    bp, tp, _ = x_prompt.shape
    bs, ts, _ = x_sample.shape
    n_mem = mem_prompt.shape[1]
    plen = cache_sb_k.shape[2]

    def vec(a):
        return a.reshape(a.shape[0], 1, -1)

    def per_head(a):
        return jnp.tile(a, (1, SB_HEADS))

    w_in_b = w_in.astype(BF16)
    P = dict(
        ffn1_norm=vec(ffn1_norm), ffn1_wg=ffn1_wg.astype(BF16), ffn1_wu=ffn1_wu.astype(BF16),
        ffn1_wd=ffn1_wd.astype(BF16),
        mix_norm=vec(mix_norm), w_rq=w_in_b[:, :, 0:_K0],
        w_kvt=jnp.swapaxes(w_in_b[:, :, _K0:_C0], 1, 2), w_cv=w_in_b[:, :, _C0:],
        w_out=w_out.astype(BF16),
        rw_mu_rkv=rw_mu_rkv.reshape(depth, 1, _RKV), rw_mu_wag=rw_mu_wag,
        rw_w0=vec(rw_w0), rw_w1=rw_w1.astype(BF16), rw_w2=rw_w2.astype(BF16),
        rw_a0=vec(rw_a0), rw_a1=rw_a1.astype(BF16), rw_a2=rw_a2.astype(BF16),
        rw_g1=rw_g1.astype(BF16), rw_g2=rw_g2.astype(BF16),
        rw_k_k=vec(rw_k_k), rw_k_a=vec(rw_k_a), rw_r_k=rw_r_k.reshape(depth, 1, RW_DIM),
        rw_ln_w=vec(rw_ln_w), rw_ln_b=vec(rw_ln_b),
        sb_q_norm=vec(per_head(sb_q_norm)), sb_k_norm_col=per_head(sb_k_norm).reshape(depth, SB_DIM, 1), cv_w=cv_w,
        x_norm=vec(x_norm), mem_norm=vec(mem_norm),
        x_wq=x_wq.astype(BF16), x_wk=x_wk.astype(BF16), x_wv=x_wv.astype(BF16), x_wo=x_wo.astype(BF16),
        x_q_norm=vec(x_q_norm), x_k_norm=vec(x_k_norm),
        ffn2_norm=vec(ffn2_norm), ffn2_wg=ffn2_wg.astype(BF16), ffn2_wu=ffn2_wu.astype(BF16),
        ffn2_wd=ffn2_wd.astype(BF16),
    )
    C = {}
    for tm in {min(RWKV_TM, bp * tp), min(RWKV_TM, bs * ts)}:
        C["cumtot%d" % tm] = _chunk_sum_matrix(tm)
    for size in {min(SB_TK, tp), min(SB_TK, ts), SB_TK}:
        C["low%d" % size] = jnp.tile(_lower_ones(size, False), (2, 1))

    dt = x_prompt.dtype
    zero_s = jnp.zeros((bp, RW_HEADS, HEAD_DIM, HEAD_DIM), state_rwkv.dtype)
    zero_shift = jnp.zeros((bp, 1, D_MODEL), dt)
    zero_rkv = jnp.zeros((bp, 1, _RKV), dt)
    zero_conv = jnp.zeros((bp, CONV_W - 1, CONV_DIM), dt)
    past_kt = jnp.transpose(cache_sb_k, (0, 1, 3, 4, 2)).reshape(depth, bs, SB_DIM, plen)
    past_vt = jnp.transpose(cache_sb_v, (0, 1, 3, 4, 2)).reshape(depth, bs, SB_DIM, plen)
    cmk = cache_mem_k.reshape(depth, bs, n_mem, D_MODEL)
    cmv = cache_mem_v.reshape(depth, bs, n_mem, D_MODEL)

    yp = x_prompt.reshape(bp * tp, D_MODEL)
    ys = x_sample.reshape(bs * ts, D_MODEL)
    mem = mem_prompt.reshape(bp * n_mem, D_MODEL)
    pkt = jnp.zeros((depth, bp, SB_DIM, tp), dt)
    pvt = jnp.zeros((depth, bp, SB_DIM, tp), dt)
    skt = jnp.zeros((depth, 1, SB_DIM, bs * ts), dt)
    svt = jnp.zeros((depth, 1, SB_DIM, bs * ts), dt)
    pmk, pmv, pmk2, pmv2 = _memory_kv(mem, P)
    pmk = pmk.reshape(depth, bp, n_mem, X_HEADS, X_HEAD_DIM)
    pmv = pmv.reshape(depth, bp, n_mem, X_HEADS, X_HEAD_DIM)
    pmk2 = pmk2.reshape(depth, bp, n_mem, D_MODEL)
    pmv2 = pmv2.reshape(depth, bp, n_mem, D_MODEL)
    pS, psh, pcv = [], [], []
    sS, ssh, scv = [], [], []
    for l in range(depth):
        yp, s_t, sh, cbuf, pkt, pvt = _layer(yp, l, bp, pmk2, pmv2, l, None, None, zero_s, zero_shift, zero_rkv,
                                             zero_conv, pkt, pvt, P, C)
        pS.append(s_t); psh.append(sh); pcv.append(cbuf)
        shift_l = state_shift[l]
        rkvprev = _state_proj(shift_l, l, P["w_rq"]).reshape(bs, 1, _RKV)
        ys, s_t, sh, cbuf, skt, svt = _layer(ys, l, bs, cmk, cmv, l, past_kt, past_vt, state_rwkv[l],
                                             shift_l.reshape(bs, 1, D_MODEL), rkvprev, state_conv[l], skt, svt, P, C)
        sS.append(s_t); ssh.append(sh); scv.append(cbuf)

    def stack(xs, shape):
        return jnp.stack(xs).reshape((depth,) + shape)

    def prompt_kv(a):
        return jnp.transpose(a.reshape(depth, bp, SB_HEADS, HEAD_DIM, tp), (0, 1, 4, 2, 3))

    def sample_kv(a):
        return jnp.transpose(a.reshape(depth, SB_HEADS, HEAD_DIM, bs, ts), (0, 3, 4, 1, 2))

    return (
        yp.reshape(bp, tp, D_MODEL), ys.reshape(bs, ts, D_MODEL),
        prompt_kv(pkt), prompt_kv(pvt),
        stack(pS, (bp, RW_HEADS, HEAD_DIM, HEAD_DIM)), stack(psh, (bp, D_MODEL)),
        stack(pcv, (bp, CONV_W - 1, CONV_DIM)),
        pmk, pmv,
        sample_kv(skt), sample_kv(svt),
        stack(sS, (bs, RW_HEADS, HEAD_DIM, HEAD_DIM)), stack(ssh, (bs, D_MODEL)),
        stack(scv, (bs, CONV_W - 1, CONV_DIM)),
    )
```

```python
import functools

import jax
import jax.numpy as jnp
from jax import lax
from jax.experimental import pallas as pl
from jax.experimental.pallas import tpu as pltpu

F32 = jnp.float32
BF16 = jnp.bfloat16

D_MODEL = 1024
HEAD_DIM = 64
RW_HEADS = 6
SB_HEADS = 6
RW_DIM = RW_HEADS * HEAD_DIM
SB_DIM = SB_HEADS * HEAD_DIM
CONV_DIM = 256
CONV_W = 3
FFN_DIM = 2816
X_HEADS = 4
X_HEAD_DIM = D_MODEL // X_HEADS
RMS_EPS = 1e-6
GN_EPS = 64e-5
L2_EPS = 1e-12
LOG2E = 1.4426950408889634

CHUNK = 64
LANES = 128
SUBLANES = 8
PAIRS = RW_DIM // LANES
VMEM_LIMIT = 56 * 1024 * 1024

FFN_TM = 1024
FFN_TF = 256
MIX_TM = 512
RWKV_TM = 512
SB_TQ = 128
SB_TK = 256
SB_CHAINS = 2
XA_TQ = 512
MEM_TM = 512

_RKV = 3 * RW_DIM
_Q0 = _RKV
_K0 = _Q0 + SB_DIM
_V0 = _K0 + SB_DIM
_C0 = _V0 + SB_DIM


def _dot(a, b):
    return jnp.dot(a, b, preferred_element_type=F32)


def _dot_nt(a, b):
    return lax.dot_general(a, b, (((1,), (1,)), ((), ())), preferred_element_type=F32)


def _dot_tn(a, b):
    return lax.dot_general(a, b, (((0,), (0,)), ((), ())), preferred_element_type=F32)


def _split(x, terms):
    parts = []
    rem = x
    for t in range(terms):
        p = rem.astype(BF16)
        parts.append(p)
        if t + 1 < terms:
            rem = rem - p.astype(F32)
    return parts


def _dot_exact_rhs(x, m, terms=3):
    acc = None
    for p in _split(x, terms):
        y = _dot(p, m)
        acc = y if acc is None else acc + y
    return acc


def _dot_exact_lhs(m, x, terms=3):
    acc = None
    for p in _split(x, terms):
        y = _dot(m, p)
        acc = y if acc is None else acc + y
    return acc


def _head_sums_lanes(x):
    lane_lo = _lane_lo()
    outs = []
    for p in range(x.shape[1] // LANES):
        xp = x[:, p * LANES:(p + 1) * LANES]
        lo = jnp.sum(jnp.where(lane_lo, xp, 0.0), axis=1, keepdims=True)
        hi = jnp.sum(jnp.where(lane_lo, 0.0, xp), axis=1, keepdims=True)
        outs.append(jnp.where(lane_lo, lo, hi))
    return jnp.concatenate(outs, axis=1)


def _rms(x, g):
    return x * lax.rsqrt(jnp.mean(x * x, axis=-1, keepdims=True) + RMS_EPS) * g


def _sigmoid(x):
    return 1.0 / (1.0 + jnp.exp(-x))


def _softplus(y):
    return jnp.maximum(y, 0.0) + jnp.log1p(jnp.exp(-jnp.abs(y)))


def _params(sem):
    return pltpu.CompilerParams(dimension_semantics=sem, vmem_limit_bytes=VMEM_LIMIT)


def _lspec(l, tail):
    nz = (0,) * len(tail)
    return pl.BlockSpec((None,) + tuple(tail), lambda *_: (l,) + nz)


def _lane_lo():
    return lax.broadcasted_iota(jnp.int32, (1, LANES), 1) < HEAD_DIM


def _pair_stack(xp, lane_lo):
    return jnp.concatenate([jnp.where(lane_lo, xp, 0.0), jnp.where(lane_lo, 0.0, xp)], axis=0)


def _ffn_body(x_ref, g_ref, wg_ref, wu_ref, wd_ref, o_ref):
    x = x_ref[...]
    h = _rms(x, g_ref[...]).astype(BF16)
    acc = None
    for f in range(FFN_DIM // FFN_TF):
        fs = slice(f * FFN_TF, (f + 1) * FFN_TF)
        gate = _dot(h, wg_ref[:, fs])
        up = _dot(h, wu_ref[:, fs])
        act = (gate * _sigmoid(gate) * up).astype(BF16)
        part = _dot(act, wd_ref[fs, :])
        acc = part if acc is None else acc + part
    o_ref[...] = x + 0.5 * acc


def _ffn(x, l, norm, wg, wu, wd):
    n = x.shape[0]
    tm = min(FFN_TM, n)
    once = pl.Buffered(1)
    return pl.pallas_call(
        _ffn_body,
        grid=(n // tm,),
        in_specs=[
            pl.BlockSpec((tm, D_MODEL), lambda i: (i, 0)),
            _lspec(l, (1, D_MODEL)),
            pl.BlockSpec((None, D_MODEL, FFN_DIM), lambda i: (l, 0, 0), pipeline_mode=once),
            pl.BlockSpec((None, D_MODEL, FFN_DIM), lambda i: (l, 0, 0), pipeline_mode=once),
            pl.BlockSpec((None, FFN_DIM, D_MODEL), lambda i: (l, 0, 0), pipeline_mode=once),
        ],
        out_specs=pl.BlockSpec((tm, D_MODEL), lambda i: (i, 0)),
        out_shape=jax.ShapeDtypeStruct((n, D_MODEL), F32),
        compiler_params=_params(("arbitrary",)),
        name="ffn",
    )(x, norm, wg, wu, wd)


def _mix_body(x_ref, g_ref, wrq_ref, wkv_ref, wcv_ref, qg_ref, kg_ref, cw_ref, c0_ref, kin_ref, vin_ref,
              h_ref, rkv_ref, q_ref, ocv_ref, cbuf_ref, kt_ref, vt_ref, tail_scr, *, tm, rs):
    del kin_ref, vin_ref
    i = pl.program_id(0)
    h = _rms(x_ref[...], g_ref[...])
    h_ref[...] = h
    hb = h.astype(BF16)
    rkv_ref[...] = _dot(hb, wrq_ref[:, 0:_RKV])
    sq = _dot(hb, wrq_ref[:, _RKV:_RKV + SB_DIM])
    ms = _head_sums_lanes(sq * sq) * (1.0 / HEAD_DIM)
    q_ref[...] = sq * lax.rsqrt(ms + RMS_EPS) * qg_ref[...] * (HEAD_DIM ** -0.5 * LOG2E)

    kt = _dot_nt(wkv_ref[0:SB_DIM, :], hb)
    k3 = kt.reshape(SB_HEADS, HEAD_DIM, tm)
    kms = jnp.mean(k3 * k3, axis=1, keepdims=True)
    kt_ref[...] = (k3 * lax.rsqrt(kms + RMS_EPS)).reshape(SB_DIM, tm) * kg_ref[...]
    vt_ref[...] = _dot_nt(wkv_ref[SB_DIM:2 * SB_DIM, :], hb)

    cb = _dot(hb, wcv_ref[:, 0:CONV_DIM])
    u = _dot(hb, wcv_ref[:, CONV_DIM:2 * CONV_DIM]) * _dot(hb, wcv_ref[:, 2 * CONV_DIM:3 * CONV_DIM])
    w0 = cw_ref[0:1, :]
    w1 = cw_ref[1:2, :]
    w2 = cw_ref[2:3, :]
    seg = min(tm, rs)
    nseg = tm // seg
    if rs >= tm:
        tiles_per_seq = rs // tm

        @pl.when(i % tiles_per_seq == 0)
        def _():
            tail_scr[SUBLANES - 2:SUBLANES, :] = c0_ref[0]

    row = lax.broadcasted_iota(jnp.int32, (seg, CONV_DIM), 0)
    for s in range(nseg):
        us = u[s * seg:(s + 1) * seg]
        if rs >= tm:
            b0 = tail_scr[SUBLANES - 2:SUBLANES - 1, :]
            b1 = tail_scr[SUBLANES - 1:SUBLANES, :]
        else:
            b0 = c0_ref[s, 0:1, :]
            b1 = c0_ref[s, 1:2, :]
        p1 = jnp.where(row == 0, b1, pltpu.roll(us, 1, 0))
        p2 = jnp.where(row == 0, b0, jnp.where(row == 1, b1, pltpu.roll(us, 2, 0)))
        y = p2 * w0 + p1 * w1 + us * w2
        ocv_ref[s * seg:(s + 1) * seg, :] = cb[s * seg:(s + 1) * seg] * y
        tail_scr[...] = us[seg - SUBLANES:seg]
        cbuf_ref[s] = tail_scr[SUBLANES - 2:SUBLANES, :]


def _mix_proj(x, l, rs, P, C, conv0, kt_all, vt_all):
    n = x.shape[0]
    tm = min(MIX_TM, n)
    nseq = n // rs
    if rs >= tm:
        tps = rs // tm
        nsb = 1
        smap = lambda i: (i // tps, 0, 0)
        tmap = lambda i: (l, i // tps, 0, i % tps)
    else:
        nsb = tm // rs
        smap = lambda i: (i, 0, 0)
        tmap = lambda i: (l, 0, 0, i)
    row = lambda i: (i, 0)
    tspec = pl.BlockSpec((None, None, SB_DIM, tm), tmap)
    return pl.pallas_call(
        functools.partial(_mix_body, tm=tm, rs=rs),
        grid=(n // tm,),
        in_specs=[
            pl.BlockSpec((tm, D_MODEL), row),
            _lspec(l, (1, D_MODEL)),
            _lspec(l, (D_MODEL, _RKV + SB_DIM)),
            _lspec(l, (2 * SB_DIM, D_MODEL)),
            _lspec(l, (D_MODEL, 3 * CONV_DIM)),
            _lspec(l, (1, SB_DIM)),
            _lspec(l, (SB_DIM, 1)),
            _lspec(l, (CONV_W, CONV_DIM)),
            pl.BlockSpec((nsb, CONV_W - 1, CONV_DIM), smap),
            pl.BlockSpec(memory_space=pl.ANY),
            pl.BlockSpec(memory_space=pl.ANY),
        ],
        out_specs=[
            pl.BlockSpec((tm, D_MODEL), row),
            pl.BlockSpec((tm, _RKV), row),
            pl.BlockSpec((tm, SB_DIM), row),
            pl.BlockSpec((tm, CONV_DIM), row),
            pl.BlockSpec((nsb, CONV_W - 1, CONV_DIM), smap),
            tspec,
            tspec,
        ],
        out_shape=[
            jax.ShapeDtypeStruct((n, D_MODEL), F32),
            jax.ShapeDtypeStruct((n, _RKV), F32),
            jax.ShapeDtypeStruct((n, SB_DIM), F32),
            jax.ShapeDtypeStruct((n, CONV_DIM), F32),
            jax.ShapeDtypeStruct((nseq, CONV_W - 1, CONV_DIM), F32),
            jax.ShapeDtypeStruct(kt_all.shape, F32),
            jax.ShapeDtypeStruct(vt_all.shape, F32),
        ],
        input_output_aliases={9: 5, 10: 6},
        scratch_shapes=[pltpu.VMEM((SUBLANES, CONV_DIM), F32)],
        compiler_params=_params(("arbitrary",)),
        name="mix_proj",
    )(x, P["mix_norm"], P["w_rq"], P["w_kvt"], P["w_cv"], P["sb_q_norm"], P["sb_k_norm_col"],
      P["cv_w"], conv0, kt_all, vt_all)


def _state_proj_body(s_ref, w_ref, o_ref):
    o_ref[...] = _dot(s_ref[...].astype(BF16), w_ref[:, 0:_RKV])


def _state_proj(shift0, l, w_rq):
    b = shift0.shape[0]
    return pl.pallas_call(
        _state_proj_body,
        grid=(1,),
        in_specs=[pl.BlockSpec((b, D_MODEL), lambda i: (0, 0)), _lspec(l, (D_MODEL, _RKV + SB_DIM))],
        out_specs=pl.BlockSpec((b, _RKV), lambda i: (0, 0)),
        out_shape=jax.ShapeDtypeStruct((b, _RKV), F32),
        compiler_params=_params(("arbitrary",)),
        name="state_proj",
    )(shift0, w_rq)


def _unit_lower_inverses(lmats):
    n = lmats[0].shape[0]
    eye = (lax.broadcasted_iota(jnp.int32, (n, n), 0) == lax.broadcasted_iota(jnp.int32, (n, n), 1)).astype(F32)
    ps = [eye + lm for lm in lmats]
    lks = []
    for lm in lmats:
        lb = lm.astype(BF16)
        lks.append(_dot(lb, lb))
    power = 2
    while power < CHUNK:
        for t in range(len(lmats)):
            lkb = lks[t].astype(BF16)
            if 2 * power < CHUNK:
                both = _dot(lkb, jnp.concatenate([lks[t], ps[t]], axis=1).astype(BF16))
                lks[t] = both[:, 0:n]
                ps[t] = ps[t] + both[:, n:2 * n]
            else:
                ps[t] = ps[t] + _dot(lkb, ps[t].astype(BF16))
        power *= 2
    return ps


def _state_to_pairs(s_ref, sq):
    zero = jnp.zeros((HEAD_DIM, HEAD_DIM), F32)
    out = []
    for p in range(PAIRS):
        top = jnp.concatenate([s_ref[sq, 2 * p], zero], axis=1)
        bot = jnp.concatenate([zero, s_ref[sq, 2 * p + 1]], axis=1)
        out.append(jnp.concatenate([top, bot], axis=0))
    return out


def _pairs_to_state(sbd, sout_ref, sq):
    for p in range(PAIRS):
        sout_ref[sq, 2 * p] = sbd[p][0:HEAD_DIM, 0:HEAD_DIM]
        sout_ref[sq, 2 * p + 1] = sbd[p][HEAD_DIM:LANES, HEAD_DIM:LANES]


def _rwkv_body(h_ref, rkv_ref, sh0_ref, rp0_ref, s0_ref,
               murkv_ref, muwag_ref, w0_ref, w1_ref, w2_ref, a0_ref, a1_ref, a2_ref, g1_ref, g2_ref,
               kk_ref, ka_ref, rk_ref, lnw_ref, lnb_ref, csum_ref,
               o_ref, sout_ref, hl_scr, rl_scr, s_scr, *, tm, rs):
    i = pl.program_id(0)
    seg = min(tm, rs)
    nseg = tm // seg
    nchunk = tm // CHUNK
    carried = rs >= tm
    if carried:
        tiles_per_seq = rs // tm

        @pl.when(i % tiles_per_seq == 0)
        def _():
            hl_scr[SUBLANES - 1:SUBLANES, :] = sh0_ref[0]
            rl_scr[SUBLANES - 1:SUBLANES, :] = rp0_ref[0]
            init = _state_to_pairs(s0_ref, 0)
            for p in range(PAIRS):
                s_scr[p] = init[p]

    h = h_ref[...]
    rkv = rkv_ref[...]
    if carried:
        hprev_rows = hl_scr[SUBLANES - 1:SUBLANES, :]
        rprev_rows = rl_scr[SUBLANES - 1:SUBLANES, :]
    else:
        hprev_rows = jnp.concatenate(
            [jnp.broadcast_to(sh0_ref[s], (seg, D_MODEL)) for s in range(nseg)], axis=0)
        rprev_rows = jnp.concatenate(
            [jnp.broadcast_to(rp0_ref[s], (seg, _RKV)) for s in range(nseg)], axis=0)
    first_h = (lax.broadcasted_iota(jnp.int32, (tm, D_MODEL), 0) & (seg - 1)) == 0
    first_r = (lax.broadcasted_iota(jnp.int32, (tm, _RKV), 0) & (seg - 1)) == 0
    hp = jnp.where(first_h, hprev_rows, pltpu.roll(h, 1, 0))
    rp = jnp.where(first_r, rprev_rows, pltpu.roll(rkv, 1, 0))
    if carried:
        hl_scr[...] = h[tm - SUBLANES:tm]
        rl_scr[...] = rkv[tm - SUBLANES:tm]

    rkv = rkv + murkv_ref[...] * (rp - rkv)
    r = rkv[:, 0:RW_DIM]
    k = rkv[:, RW_DIM:2 * RW_DIM]
    v = rkv[:, 2 * RW_DIM:3 * RW_DIM]
    dx = hp - h
    xw = (h + muwag_ref[0:1, :] * dx).astype(BF16)
    xa = (h + muwag_ref[1:2, :] * dx).astype(BF16)
    xg = (h + muwag_ref[2:3, :] * dx).astype(BF16)
    wl = w0_ref[...] + _dot(jnp.tanh(_dot(xw, w1_ref[...])).astype(BF16), w2_ref[...])
    w = -_softplus(-wl) - 0.5
    logdecay = -jnp.exp(w)
    a = _sigmoid(a0_ref[...] + _dot(_dot(xa, a1_ref[...]).astype(BF16), a2_ref[...]))
    gate = _dot(_sigmoid(_dot(xg, g1_ref[...])).astype(BF16), g2_ref[...])
    kk = k * kk_ref[...]
    kk = kk * lax.rsqrt(_head_sums_lanes(kk * kk) + L2_EPS)
    k = k * (1.0 + (a - 1.0) * ka_ref[...])
    b = kk * a
    bonus = _head_sums_lanes(r * k * rk_ref[...]) * v

    ci = lax.broadcasted_iota(jnp.int32, (LANES, LANES), 0)
    cj = lax.broadcasted_iota(jnp.int32, (LANES, LANES), 1)
    strict = cj < ci
    incl = cj <= ci
    lane_lo = _lane_lo()

    sums = _dot_exact_lhs(csum_ref[...], logdecay, terms=2)
    cum = sums[0:tm]
    tot = sums[tm:2 * tm]
    g_incl = jnp.exp(cum)
    g_prev = jnp.exp(cum - logdecay)
    g_inv = jnp.exp(-cum)
    g_rem = jnp.exp(tot - cum)
    g_tot = jnp.exp(tot)
    rq = r * g_incl
    kq = k * g_inv
    bq = b * g_inv
    aq = -kk * g_prev
    kz = k * g_rem
    bz = b * g_rem
    items = [(c, p) for c in range(nchunk) for p in range(PAIRS)]

    def stacked(x, c, p):
        return _pair_stack(x[c * CHUNK:(c + 1) * CHUNK, p * LANES:(p + 1) * LANES], lane_lo).astype(BF16)

    work = [dict(aq=stacked(aq, c, p), rq=stacked(rq, c, p), v=stacked(v, c, p),
                 kzbz=jnp.concatenate([stacked(kz, c, p), stacked(bz, c, p)], axis=0),
                 gtot=g_tot[c * CHUNK:c * CHUNK + 1, p * LANES:(p + 1) * LANES]) for c, p in items]
    quads = [_dot_nt(jnp.concatenate([wk["aq"], wk["rq"]], axis=0),
                     jnp.concatenate([stacked(kq, c, p), stacked(bq, c, p)], axis=0))
             for wk, (c, p) in zip(work, items)]
    lmats = []
    for wk, quad in zip(work, quads):
        wk["m_ak"] = jnp.where(strict, quad[0:LANES, 0:LANES], 0.0).astype(BF16)
        lmats.append(jnp.where(strict, quad[0:LANES, LANES:2 * LANES], 0.0))
        wk["mr"] = jnp.concatenate([jnp.where(incl, quad[LANES:2 * LANES, 0:LANES], 0.0),
                                    jnp.where(incl, quad[LANES:2 * LANES, LANES:2 * LANES], 0.0)],
                                   axis=1).astype(BF16)
    for wk in work:
        wk["x1"] = _dot(wk["m_ak"], wk["v"])
    tinvs = _unit_lower_inverses(lmats)
    for wk, tinv in zip(work, tinvs):
        wu = _dot(tinv.astype(BF16), jnp.concatenate([wk["aq"], wk["x1"].astype(BF16)], axis=1))
        wk["wr"] = jnp.concatenate([wu[:, 0:LANES].astype(BF16), wk["rq"]], axis=0)
        wk["u0"] = wu[:, LANES:2 * LANES]

    if carried:
        sbd = [s_scr[p] for p in range(PAIRS)]
    o_chunks = []
    for c in range(nchunk):
        sq = (c * CHUNK) // seg
        if not carried:
            sbd = _state_to_pairs(s0_ref, sq)
        wks = work[c * PAIRS:(c + 1) * PAIRS]
        ys = [_dot_nt(wks[p]["wr"], sbd[p].astype(BF16)) for p in range(PAIRS)]
        vus = [jnp.concatenate([wks[p]["v"], (ys[p][0:LANES] + wks[p]["u0"]).astype(BF16)], axis=0)
               for p in range(PAIRS)]
        sbd = [sbd[p] * wks[p]["gtot"] + _dot_tn(vus[p], wks[p]["kzbz"]) for p in range(PAIRS)]
        o_ss = [ys[p][LANES:2 * LANES] + _dot(wks[p]["mr"], vus[p]) for p in range(PAIRS)]
        if not carried:
            _pairs_to_state(sbd, sout_ref, sq)
        o_chunks.append(jnp.concatenate([o_s[0:CHUNK] + o_s[CHUNK:2 * CHUNK] for o_s in o_ss], axis=1))
    if carried:
        for p in range(PAIRS):
            s_scr[p] = sbd[p]
        _pairs_to_state(sbd, sout_ref, 0)
    o = jnp.concatenate(o_chunks, axis=0)
    mean = _head_sums_lanes(o) * (1.0 / HEAD_DIM)
    d = o - mean
    var = _head_sums_lanes(d * d) * (1.0 / HEAD_DIM)
    on = d * lax.rsqrt(var + GN_EPS) * lnw_ref[...] + lnb_ref[...]
    o_ref[...] = (on + bonus) * gate


def _rwkv(h, rkv, l, rs, shift0, rkvprev0, s0, P, C):
    n = h.shape[0]
    tm = min(RWKV_TM, n)
    nseq = n // rs
    if rs >= tm:
        tps = rs // tm
        nsb = 1
        smap3 = lambda i: (i // tps, 0, 0)
        smap4 = lambda i: (i // tps, 0, 0, 0)
    else:
        nsb = tm // rs
        smap3 = lambda i: (i, 0, 0)
        smap4 = lambda i: (i, 0, 0, 0)
    row = lambda i: (i, 0)
    lora_w, lora_a, lora_g = P["rw_w1"].shape[-1], P["rw_a1"].shape[-1], P["rw_g1"].shape[-1]
    return pl.pallas_call(
        functools.partial(_rwkv_body, tm=tm, rs=rs),
        grid=(n // tm,),
        in_specs=[
            pl.BlockSpec((tm, D_MODEL), row),
            pl.BlockSpec((tm, _RKV), row),
            pl.BlockSpec((nsb, 1, D_MODEL), smap3),
            pl.BlockSpec((nsb, 1, _RKV), smap3),
            pl.BlockSpec((nsb, RW_HEADS, HEAD_DIM, HEAD_DIM), smap4),
            _lspec(l, (1, _RKV)),
            _lspec(l, (3, D_MODEL)),
            _lspec(l, (1, RW_DIM)),
            _lspec(l, (D_MODEL, lora_w)),
            _lspec(l, (lora_w, RW_DIM)),
            _lspec(l, (1, RW_DIM)),
            _lspec(l, (D_MODEL, lora_a)),
            _lspec(l, (lora_a, RW_DIM)),
            _lspec(l, (D_MODEL, lora_g)),
            _lspec(l, (lora_g, RW_DIM)),
            _lspec(l, (1, RW_DIM)),
            _lspec(l, (1, RW_DIM)),
            _lspec(l, (1, RW_DIM)),
            _lspec(l, (1, RW_DIM)),
            _lspec(l, (1, RW_DIM)),
            pl.BlockSpec((2 * tm, tm), lambda i: (0, 0)),
        ],
        out_specs=[
            pl.BlockSpec((tm, RW_DIM), row),
            pl.BlockSpec((nsb, RW_HEADS, HEAD_DIM, HEAD_DIM), smap4),
        ],
        out_shape=[
            jax.ShapeDtypeStruct((n, RW_DIM), F32),
            jax.ShapeDtypeStruct((nseq, RW_HEADS, HEAD_DIM, HEAD_DIM), F32),
        ],
        scratch_shapes=[
            pltpu.VMEM((SUBLANES, D_MODEL), F32),
            pltpu.VMEM((SUBLANES, _RKV), F32),
            pltpu.VMEM((PAIRS, LANES, LANES), F32),
        ],
        compiler_params=_params(("arbitrary",)),
        name="rwkv",
    )(h, rkv, shift0, rkvprev0, s0,
      P["rw_mu_rkv"], P["rw_mu_wag"], P["rw_w0"], P["rw_w1"], P["rw_w2"], P["rw_a0"], P["rw_a1"],
      P["rw_a2"], P["rw_g1"], P["rw_g2"], P["rw_k_k"], P["rw_k_a"], P["rw_r_k"], P["rw_ln_w"],
      P["rw_ln_b"], C["cumtot%d" % tm])


def _sb_log_rest(z, mask):
    log_rest = jnp.minimum(-z, 0.0) - jnp.log2(1.0 + jnp.exp2(-jnp.abs(z)))
    if mask is not None:
        log_rest = jnp.where(mask, log_rest, 0.0)
    return log_rest


def _sb_weights(z, incl, carry, mask):
    wts = jnp.exp2(z + incl + carry)
    if mask is not None:
        wts = jnp.where(mask, wts, 0.0)
    return wts.astype(BF16), carry + incl[:, 0:1]


def _sb_body(*refs, tq, nsub, nseqs, tkd, n_past, tkp):
    if n_past:
        q_ref, kc_ref, vc_ref, kp_ref, vp_ref, lowd_ref, lowp_ref, o_ref = refs
    else:
        q_ref, kc_ref, vc_ref, lowd_ref, o_ref = refs
    qi = pl.program_id(1)
    lane_lo = _lane_lo()
    rows = SB_HEADS * tq
    chains = [(g, s) for g in range(nseqs) for s in range(nsub)]
    qs = [[_pair_stack(q_ref[c * tq:(c + 1) * tq, p * LANES:(p + 1) * LANES], lane_lo).astype(BF16)
           for p in range(PAIRS)] for c in range(len(chains))]

    def block(ref, g, p, off, tk):
        cols = slice(None) if ref.shape[2] == tk else pl.ds(off, tk)
        return ref[g, p * LANES:(p + 1) * LANES, cols].astype(BF16)

    def step(k_ref, v_ref, off, tk, lower, st, masks, widths):
        kbs = [[block(k_ref, g, p, off, tk) for p in range(PAIRS)] for g in range(nseqs)]
        vbs = [[block(v_ref, g, p, off, tk) for p in range(PAIRS)] for g in range(nseqs)]
        zs = [jnp.concatenate([_dot(qs[c][p], kbs[g][p][:, 0:widths[c]]) for p in range(PAIRS)], axis=0)
              for c, (g, _) in enumerate(chains)]
        incls = []
        for c in range(len(chains)):
            w = widths[c]
            hi, lo = _split(_sb_log_rest(zs[c], masks[c]), 2)
            low = lower if w == tk else jnp.concatenate([lower[0:w, 0:w], lower[tk:tk + w, 0:w]], axis=0)
            incls.append(_dot(jnp.concatenate([hi, lo], axis=1), low))
        out = []
        for c, (g, _) in enumerate(chains):
            wts, carry = _sb_weights(zs[c], incls[c], st[2 * c], masks[c])
            pv = jnp.concatenate(
                [_dot_nt(wts[2 * tq * p:2 * tq * (p + 1)], vbs[g][p][:, 0:widths[c]]) for p in range(PAIRS)], axis=0)
            out += [carry, st[2 * c + 1] + pv]
        return tuple(out)

    lowd = lowd_ref[...]
    q0 = qi * (tq * nsub)
    jd = q0 // tkd
    offd = pl.multiple_of(jd * tkd, tkd)
    diag_w = [min(tkd, (s + 1) * tq) for _, s in chains]
    masks = []
    for (_, s), w in zip(chains, diag_w):
        kpos = offd + lax.broadcasted_iota(jnp.int32, (rows, w), 1)
        rowq = lax.broadcasted_iota(jnp.int32, (rows, w), 0) & (tq - 1)
        masks.append(kpos < (q0 + s * tq + rowq))
    st = (jnp.zeros((rows, 1), F32), jnp.zeros((rows, LANES), F32)) * len(chains)
    st = step(kc_ref, vc_ref, offd, tkd, lowd, st, masks, diag_w)
    nomask = [None] * len(chains)
    full_d = [tkd] * len(chains)
    full_p = [tkp] * len(chains)

    def cur_step(jj, st):
        off = pl.multiple_of((jd - 1 - jj) * tkd, tkd)
        return step(kc_ref, vc_ref, off, tkd, lowd, st, nomask, full_d)

    st = lax.fori_loop(0, jd, cur_step, st)
    if n_past:
        lowp = lowp_ref[...]

        def past_step(jj, st):
            off = pl.multiple_of((n_past - 1 - jj) * tkp, tkp)
            return step(kp_ref, vp_ref, off, tkp, lowp, st, nomask, full_p)

        st = lax.fori_loop(0, n_past, past_step, st)
    for c in range(len(chains)):
        acc = st[2 * c + 1]
        for p in range(PAIRS):
            base = 2 * tq * p
            o_ref[c * tq:(c + 1) * tq, p * LANES:(p + 1) * LANES] = jnp.where(
                lane_lo, acc[base:base + tq], acc[base + tq:base + 2 * tq])


def _stick_breaking(q, kt, vt, kl, nseq, past_kt, past_vt, pl_, C):
    n = q.shape[0]
    t = n // nseq
    tq = min(SB_TQ, t)
    tkd = min(SB_TK, t)
    nsub = max(1, tkd // tq)
    nseqs = max(1, SB_CHAINS // nsub)
    nq = t // (tq * nsub)
    qmap = lambda b, i: (b * nq + i, 0)
    cmap = lambda b, i: (kl, b, 0, 0)
    in_specs = [
        pl.BlockSpec((tq * nsub * nseqs, SB_DIM), qmap),
        pl.BlockSpec((None, nseqs, SB_DIM, t), cmap),
        pl.BlockSpec((None, nseqs, SB_DIM, t), cmap),
    ]
    args = [q, kt, vt]
    n_past = 0
    if past_kt is not None:
        plen = past_kt.shape[3]
        n_past = plen // SB_TK
        pmap = lambda b, i: (pl_, b, 0, 0)
        in_specs += [pl.BlockSpec((None, nseqs, SB_DIM, plen), pmap),
                     pl.BlockSpec((None, nseqs, SB_DIM, plen), pmap)]
        args += [past_kt, past_vt]
    in_specs.append(pl.BlockSpec((2 * tkd, tkd), lambda b, i: (0, 0)))
    args.append(C["low%d" % tkd])
    if n_past:
        in_specs.append(pl.BlockSpec((2 * SB_TK, SB_TK), lambda b, i: (0, 0)))
        args.append(C["low%d" % SB_TK])
    return pl.pallas_call(
        functools.partial(_sb_body, tq=tq, nsub=nsub, nseqs=nseqs, tkd=tkd, n_past=n_past, tkp=SB_TK),
        grid=(nseq // nseqs, nq),
        in_specs=in_specs,
        out_specs=pl.BlockSpec((tq * nsub * nseqs, SB_DIM), qmap),
        out_shape=jax.ShapeDtypeStruct((n, SB_DIM), F32),
        compiler_params=_params(("arbitrary", "arbitrary")),
        name="stick_breaking",
    )(*args)


def _xattn_body(x_ref, orw_ref, osb_ref, ocv_ref, wout_ref, g_ref, wq_ref, qg_ref, mk_ref, mv_ref, wo_ref, o_ref,
                *, nseqs, t):
    x = x_ref[...]
    x = x + _dot(orw_ref[...].astype(BF16), wout_ref[0:RW_DIM, :])
    x = x + _dot(osb_ref[...].astype(BF16), wout_ref[RW_DIM:RW_DIM + SB_DIM, :])
    x = x + _dot(ocv_ref[...].astype(BF16), wout_ref[RW_DIM + SB_DIM:D_MODEL, :])
    hx = _rms(x, g_ref[...]).astype(BF16)
    qf = _dot(hx, wq_ref[...])
    cols = [slice(hh * X_HEAD_DIM, (hh + 1) * X_HEAD_DIM) for hh in range(X_HEADS)]
    qhs = [_rms(qf[:, cs], qg_ref[...]).astype(BF16) for cs in cols]
    pairs = [(hh, g) for hh in range(X_HEADS) for g in range(nseqs)]
    scores = [_dot_nt(qhs[hh][g * t:(g + 1) * t], mk_ref[g, :, cols[hh]].astype(BF16)) * (X_HEAD_DIM ** -0.5)
              for hh, g in pairs]
    attns = []
    for s in scores:
        s = jnp.exp(s - jnp.max(s, axis=-1, keepdims=True))
        attns.append((s / jnp.sum(s, axis=-1, keepdims=True)).astype(BF16))
    outs = [_dot(a, mv_ref[g, :, cols[hh]].astype(BF16)) for a, (hh, g) in zip(attns, pairs)]
    heads = [outs[hh * nseqs] if nseqs == 1 else jnp.concatenate(outs[hh * nseqs:(hh + 1) * nseqs], axis=0)
             for hh in range(X_HEADS)]
    o = jnp.concatenate(heads, axis=1).astype(BF16)
    o_ref[...] = x + _dot(o, wo_ref[...])


def _xattn(x, orw, osb, ocv, l, nseq, P, mem_k, mem_v, mem_l):
    n = x.shape[0]
    rs = n // nseq
    t = min(XA_TQ, rs)
    nseqs = max(1, XA_TQ // rs)
    tq = t * nseqs
    nq = rs // t
    nm = mem_k.shape[2]
    row = lambda b, i: (b * nq + i, 0)
    mmap = lambda b, i: (mem_l, b, 0, 0)
    return pl.pallas_call(
        functools.partial(_xattn_body, nseqs=nseqs, t=t),
        grid=(nseq // nseqs, nq),
        in_specs=[
            pl.BlockSpec((tq, D_MODEL), row),
            pl.BlockSpec((tq, RW_DIM), row),
            pl.BlockSpec((tq, SB_DIM), row),
            pl.BlockSpec((tq, CONV_DIM), row),
            _lspec(l, (D_MODEL, D_MODEL)),
            _lspec(l, (1, D_MODEL)),
            _lspec(l, (D_MODEL, D_MODEL)),
            _lspec(l, (1, X_HEAD_DIM)),
            pl.BlockSpec((None, nseqs, nm, D_MODEL), mmap),
            pl.BlockSpec((None, nseqs, nm, D_MODEL), mmap),
            _lspec(l, (D_MODEL, D_MODEL)),
        ],
        out_specs=pl.BlockSpec((tq, D_MODEL), row),
        out_shape=jax.ShapeDtypeStruct((n, D_MODEL), F32),
        compiler_params=_params(("arbitrary", "arbitrary")),
        name="xattn",
    )(x, orw, osb, ocv, P["w_out"], P["x_norm"], P["x_wq"], P["x_q_norm"], mem_k, mem_v, P["x_wo"])


def _memkv_body(m_ref, g_ref, wk_ref, wv_ref, kg_ref, mk4_ref, mv4_ref, mk_ref, mv_ref):
    m = _rms(m_ref[...], g_ref[...]).astype(BF16)
    kf = _dot(m, wk_ref[...])
    vf = _dot(m, wv_ref[...])
    mv_ref[...] = vf
    for hh in range(X_HEADS):
        cs = slice(hh * X_HEAD_DIM, (hh + 1) * X_HEAD_DIM)
        kn = _rms(kf[:, cs], kg_ref[...])
        mk_ref[:, cs] = kn
        mk4_ref[:, hh, :] = kn
        mv4_ref[:, hh, :] = vf[:, cs]


def _memory_kv(mem, P):
    n = mem.shape[0]
    depth = P["x_wk"].shape[0]
    tm = min(MEM_TM, n)
    lay = lambda l, i: (l, 0, 0)
    out4 = pl.BlockSpec((None, tm, X_HEADS, X_HEAD_DIM), lambda l, i: (l, i, 0, 0))
    out2 = pl.BlockSpec((None, tm, D_MODEL), lambda l, i: (l, i, 0))
    shape4 = jax.ShapeDtypeStruct((depth, n, X_HEADS, X_HEAD_DIM), F32)
    shape2 = jax.ShapeDtypeStruct((depth, n, D_MODEL), F32)
    return pl.pallas_call(
        _memkv_body,
        grid=(depth, n // tm),
        in_specs=[
            pl.BlockSpec((tm, D_MODEL), lambda l, i: (i, 0)),
            pl.BlockSpec((None, 1, D_MODEL), lay),
            pl.BlockSpec((None, D_MODEL, D_MODEL), lay),
            pl.BlockSpec((None, D_MODEL, D_MODEL), lay),
            pl.BlockSpec((None, 1, X_HEAD_DIM), lay),
        ],
        out_specs=[out4, out4, out2, out2],
        out_shape=[shape4, shape4, shape2, shape2],
        compiler_params=_params(("arbitrary", "arbitrary")),
        name="memory_kv",
    )(mem, P["mem_norm"], P["x_wk"], P["x_wv"], P["x_k_norm"])


def _layer(x, l, nseq, mem_k, mem_v, mem_l, past_kt, past_vt, s0, shift0, rkvprev0, conv0, kt_all, vt_all, P, C):
    n = x.shape[0]
    rs = n // nseq
    x = _ffn(x, l, P["ffn1_norm"], P["ffn1_wg"], P["ffn1_wu"], P["ffn1_wd"])
    h, rkv, q, o_cv, conv_buf, kt_all, vt_all = _mix_proj(x, l, rs, P, C, conv0, kt_all, vt_all)
    o_rw, s_t = _rwkv(h, rkv, l, rs, shift0, rkvprev0, s0, P, C)
    if kt_all.shape[1] == nseq:
        kt_cur, vt_cur, kl = kt_all, vt_all, l
    else:
        def per_seq(a):
            return a[l, 0].reshape(SB_DIM, nseq, rs).transpose(1, 0, 2)[None]
        kt_cur, vt_cur, kl = per_seq(kt_all), per_seq(vt_all), 0
    o_sb = _stick_breaking(q, kt_cur, vt_cur, kl, nseq, past_kt, past_vt, l, C)
    x = _xattn(x, o_rw, o_sb, o_cv, l, nseq, P, mem_k, mem_v, mem_l)
    x = _ffn(x, l, P["ffn2_norm"], P["ffn2_wg"], P["ffn2_wu"], P["ffn2_wd"])
    h_last = h.reshape(nseq, rs, D_MODEL)[:, rs - 1]
    return x, s_t, h_last, conv_buf, kt_all, vt_all


def _chunk_sum_matrix(tm):
    i = lax.broadcasted_iota(jnp.int32, (tm, tm), 0)
    j = lax.broadcasted_iota(jnp.int32, (tm, tm), 1)
    same = (i // CHUNK) == (j // CHUNK)
    return jnp.concatenate([same & (j <= i), same], axis=0).astype(BF16)


def _lower_ones(n, strict):
    i = lax.broadcasted_iota(jnp.int32, (n, n), 0)
    j = lax.broadcasted_iota(jnp.int32, (n, n), 1)
    return ((j < i) if strict else (j <= i)).astype(BF16)


def kernel(x_prompt, x_sample, mem_prompt, cache_sb_k, cache_sb_v, state_rwkv, state_shift, state_conv,
           cache_mem_k, cache_mem_v, ffn1_norm, ffn1_wg, ffn1_wu, ffn1_wd, mix_norm, w_in, w_out, rw_mu_rkv,
           rw_mu_wag, rw_w0, rw_w1, rw_w2, rw_a0, rw_a1, rw_a2, rw_g1, rw_g2, rw_k_k, rw_k_a, rw_r_k, rw_ln_w,
           rw_ln_b, sb_q_norm, sb_k_norm, cv_w, x_norm, mem_norm, x_wq, x_wk, x_wv, x_wo, x_q_norm, x_k_norm,
           ffn2_norm, ffn2_wg, ffn2_wu, ffn2_wd):
    depth = w_in.shape[0]
    bp, tp, _ = x_prompt.shape
    bs, ts, _ = x_sample.shape
    n_mem = mem_prompt.shape[1]
    plen = cache_sb_k.shape[2]

    def vec(a):
        return a.reshape(a.shape[0], 1, -1)

    def per_head(a):
        return jnp.tile(a, (1, SB_HEADS))

    w_in_b = w_in.astype(BF16)
    P = dict(
        ffn1_norm=vec(ffn1_norm), ffn1_wg=ffn1_wg.astype(BF16), ffn1_wu=ffn1_wu.astype(BF16),
        ffn1_wd=ffn1_wd.astype(BF16),
        mix_norm=vec(mix_norm), w_rq=w_in_b[:, :, 0:_K0],
        w_kvt=jnp.swapaxes(w_in_b[:, :, _K0:_C0], 1, 2), w_cv=w_in_b[:, :, _C0:],
        w_out=w_out.astype(BF16),
        rw_mu_rkv=rw_mu_rkv.reshape(depth, 1, _RKV), rw_mu_wag=rw_mu_wag,
        rw_w0=vec(rw_w0), rw_w1=rw_w1.astype(BF16), rw_w2=rw_w2.astype(BF16),
        rw_a0=vec(rw_a0), rw_a1=rw_a1.astype(BF16), rw_a2=rw_a2.astype(BF16),
        rw_g1=rw_g1.astype(BF16), rw_g2=rw_g2.astype(BF16),
        rw_k_k=vec(rw_k_k), rw_k_a=vec(rw_k_a), rw_r_k=rw_r_k.reshape(depth, 1, RW_DIM),
        rw_ln_w=vec(rw_ln_w), rw_ln_b=vec(rw_ln_b),
        sb_q_norm=vec(per_head(sb_q_norm)), sb_k_norm_col=per_head(sb_k_norm).reshape(depth, SB_DIM, 1), cv_w=cv_w,
        x_norm=vec(x_norm), mem_norm=vec(mem_norm),
        x_wq=x_wq.astype(BF16), x_wk=x_wk.astype(BF16), x_wv=x_wv.astype(BF16), x_wo=x_wo.astype(BF16),
        x_q_norm=vec(x_q_norm), x_k_norm=vec(x_k_norm),
        ffn2_norm=vec(ffn2_norm), ffn2_wg=ffn2_wg.astype(BF16), ffn2_wu=ffn2_wu.astype(BF16),
        ffn2_wd=ffn2_wd.astype(BF16),
    )
    C = {}
    for tm in {min(RWKV_TM, bp * tp), min(RWKV_TM, bs * ts)}:
        C["cumtot%d" % tm] = _chunk_sum_matrix(tm)
    for size in {min(SB_TK, tp), min(SB_TK, ts), SB_TK}:
        C["low%d" % size] = jnp.tile(_lower_ones(size, False), (2, 1))

    dt = x_prompt.dtype
    zero_s = jnp.zeros((bp, RW_HEADS, HEAD_DIM, HEAD_DIM), state_rwkv.dtype)
    zero_shift = jnp.zeros((bp, 1, D_MODEL), dt)
    zero_rkv = jnp.zeros((bp, 1, _RKV), dt)
    zero_conv = jnp.zeros((bp, CONV_W - 1, CONV_DIM), dt)
    past_kt = jnp.transpose(cache_sb_k, (0, 1, 3, 4, 2)).reshape(depth, bs, SB_DIM, plen)
    past_vt = jnp.transpose(cache_sb_v, (0, 1, 3, 4, 2)).reshape(depth, bs, SB_DIM, plen)
    cmk = cache_mem_k.reshape(depth, bs, n_mem, D_MODEL)
    cmv = cache_mem_v.reshape(depth, bs, n_mem, D_MODEL)

    yp = x_prompt.reshape(bp * tp, D_MODEL)
    ys = x_sample.reshape(bs * ts, D_MODEL)
    mem = mem_prompt.reshape(bp * n_mem, D_MODEL)
    pkt = jnp.zeros((depth, bp, SB_DIM, tp), dt)
    pvt = jnp.zeros((depth, bp, SB_DIM, tp), dt)
    skt = jnp.zeros((depth, 1, SB_DIM, bs * ts), dt)
    svt = jnp.zeros((depth, 1, SB_DIM, bs * ts), dt)
    pmk, pmv, pmk2, pmv2 = _memory_kv(mem, P)
    pmk = pmk.reshape(depth, bp, n_mem, X_HEADS, X_HEAD_DIM)
    pmv = pmv.reshape(depth, bp, n_mem, X_HEADS, X_HEAD_DIM)
    pmk2 = pmk2.reshape(depth, bp, n_mem, D_MODEL)
    pmv2 = pmv2.reshape(depth, bp, n_mem, D_MODEL)
    pS, psh, pcv = [], [], []
    sS, ssh, scv = [], [], []
    for l in range(depth):
        yp, s_t, sh, cbuf, pkt, pvt = _layer(yp, l, bp, pmk2, pmv2, l, None, None, zero_s, zero_shift, zero_rkv,
                                             zero_conv, pkt, pvt, P, C)
        pS.append(s_t); psh.append(sh); pcv.append(cbuf)
        shift_l = state_shift[l]
        rkvprev = _state_proj(shift_l, l, P["w_rq"]).reshape(bs, 1, _RKV)
        ys, s_t, sh, cbuf, skt, svt = _layer(ys, l, bs, cmk, cmv, l, past_kt, past_vt, state_rwkv[l],
                                             shift_l.reshape(bs, 1, D_MODEL), rkvprev, state_conv[l], skt, svt, P, C)
        sS.append(s_t); ssh.append(sh); scv.append(cbuf)

    def stack(xs, shape):
        return jnp.stack(xs).reshape((depth,) + shape)

    def prompt_kv(a):
        return jnp.transpose(a.reshape(depth, bp, SB_HEADS, HEAD_DIM, tp), (0, 1, 4, 2, 3))

    def sample_kv(a):
        return jnp.transpose(a.reshape(depth, SB_HEADS, HEAD_DIM, bs, ts), (0, 3, 4, 1, 2))

    return (
        yp.reshape(bp, tp, D_MODEL), ys.reshape(bs, ts, D_MODEL),
        prompt_kv(pkt), prompt_kv(pvt),
        stack(pS, (bp, RW_HEADS, HEAD_DIM, HEAD_DIM)), stack(psh, (bp, D_MODEL)),
        stack(pcv, (bp, CONV_W - 1, CONV_DIM)),
        pmk, pmv,
        sample_kv(skt), sample_kv(svt),
        stack(sS, (bs, RW_HEADS, HEAD_DIM, HEAD_DIM)), stack(ssh, (bs, D_MODEL)),
        stack(scv, (bs, CONV_W - 1, CONV_DIM)),
    )
```
